```python
import math
import jax, jax.numpy as jnp
from jax import lax
import numpy as np

D_MODEL = 1024
BATCH = 8
SEQ = 8192
DEPTH = 4

CHUNK = 64
Q_BLOCK = 128
HEAD_DIM = 64
A_HEADS = 4
B_HEADS = 8
C_HEADS = 16
A_WIDTH = A_HEADS * 2 * HEAD_DIM
B_WIDTH = B_HEADS * HEAD_DIM
C_WIDTH = C_HEADS * HEAD_DIM
T5_BUCKETS = 32
T5_MAX_DIST = 128
C_LEFT_CHUNKS = 8
C_MAX_REL = 128
D_FF = 2816
CONV_WIDTH = 3
N_EVEN = (DEPTH + 1) // 2
N_ODD = DEPTH // 2
EPS = 1e-6

kernel_name = 'hybrid_diff_stickbreak_chunkband_convffn'


def rms_norm(x, g):
    xf = x.astype(jnp.float32)
    y = xf * lax.rsqrt(jnp.mean(xf * xf, axis=-1, keepdims=True) + EPS)
    return (y * g.astype(jnp.float32)).astype(x.dtype)


def t5_bucket(rel):
    nb = T5_BUCKETS // 2
    max_exact = nb // 2
    n = jnp.abs(rel)
    large = max_exact + (jnp.log(jnp.maximum(n, 1).astype(jnp.float32) / max_exact)
                         / math.log(T5_MAX_DIST / max_exact) * (nb - max_exact)).astype(jnp.int32)
    large = jnp.minimum(large, nb - 1)
    return jnp.where(rel > 0, nb, 0) + jnp.where(n < max_exact, n, large)


def diff_attention(q, k, v, t5_bias, lam):
    seq = q.shape[3]
    scale = HEAD_DIM ** -0.5
    lam32 = lam.astype(jnp.float32)
    outs = []
    for i in range(seq // Q_BLOCK):
        q0 = i * Q_BLOCK
        kl = q0 + Q_BLOCK
        qpos = q0 + jnp.arange(Q_BLOCK)
        kpos = jnp.arange(kl)
        bias = t5_bias[t5_bucket(kpos[None, :] - qpos[:, None])]
        bias = jnp.transpose(bias, (2, 0, 1))[:, None].astype(jnp.float32)
        s = jnp.einsum('bhmqd,bhmkd->bhmqk', q[:, :, :, q0:kl], k[:, :, :, :kl]).astype(jnp.float32) * scale + bias
        mask = (kpos[None, :] // CHUNK) <= (qpos[:, None] // CHUNK)
        p = jax.nn.softmax(jnp.where(mask, s, -jnp.inf), axis=-1)
        w = p[:, :, 0] - lam32 * p[:, :, 1]
        outs.append(jnp.einsum('bhqk,bhkd->bhqd', w.astype(v.dtype), v[:, :, :kl]))
    return jnp.concatenate(outs, axis=2)


def stick_breaking(q, k, v):
    seq = q.shape[2]
    scale = HEAD_DIM ** -0.5
    outs = []
    for i in range(seq // Q_BLOCK):
        q0 = i * Q_BLOCK
        kl = q0 + Q_BLOCK
        qpos = q0 + jnp.arange(Q_BLOCK)
        kpos = jnp.arange(kl)
        z = jnp.einsum('bhqd,bhkd->bhqk', q[:, :, q0:kl], k[:, :, :kl]).astype(jnp.float32) * scale
        mask = kpos[None, :] < qpos[:, None]
        log_1m = jnp.where(mask, jax.nn.log_sigmoid(-z), 0.0)
        remain = lax.cumsum(log_1m, axis=3, reverse=True) - log_1m
        a = jnp.where(mask, jnp.exp(jax.nn.log_sigmoid(z) + remain), 0.0)
        outs.append(jnp.einsum('bhqk,bhkd->bhqd', a.astype(v.dtype), v[:, :, :kl]))
    return jnp.concatenate(outs, axis=2)


def chunk_band_attention(q, k, v, rel_bias):
    b, h, seq, d = q.shape
    n_chunks = seq // CHUNK
    pad = C_LEFT_CHUNKS * CHUNK
    band = (C_LEFT_CHUNKS + 1) * CHUNK
    scale = d ** -0.5
    kp = jnp.pad(k, ((0, 0), (0, 0), (pad, 0), (0, 0)))
    vp = jnp.pad(v, ((0, 0), (0, 0), (pad, 0), (0, 0)))
    qoff = jnp.arange(CHUNK)
    koff = jnp.arange(band) - pad
    rel = jnp.clip(qoff[:, None] - koff[None, :], -C_MAX_REL, C_MAX_REL) + C_MAX_REL
    bias = jnp.transpose(rel_bias[rel], (2, 0, 1)).astype(jnp.float32)

    def one_chunk(c):
        start = c * CHUNK
        qc = lax.dynamic_slice_in_dim(q, start, CHUNK, axis=2)
        kc = lax.dynamic_slice_in_dim(kp, start, band, axis=2)
        vc = lax.dynamic_slice_in_dim(vp, start, band, axis=2)
        s = jnp.einsum('bhqd,bhkd->bhqk', qc, kc).astype(jnp.float32) * scale + bias
        valid = (start + koff) >= 0
        p = jax.nn.softmax(jnp.where(valid, s, -jnp.inf), axis=-1)
        return jnp.einsum('bhqk,bhkd->bhqd', p.astype(vc.dtype), vc)

    out = lax.map(one_chunk, jnp.arange(n_chunks))
    return jnp.transpose(out, (1, 2, 0, 3, 4)).reshape(b, h, seq, d)


def even_mixer(u, w_in, q_norm, k_norm, lam_p, subln, w_out, t5_bias, lam_init):
    b, seq, _ = u.shape
    proj = u @ w_in
    aq, ak, av, bq, bk, bv = jnp.split(
        proj, [A_WIDTH, 2 * A_WIDTH, 3 * A_WIDTH, 3 * A_WIDTH + B_WIDTH, 3 * A_WIDTH + 2 * B_WIDTH], axis=-1)
    aq = rms_norm(jnp.transpose(aq.reshape(b, seq, A_HEADS, 2, HEAD_DIM), (0, 2, 3, 1, 4)), q_norm)
    ak = rms_norm(jnp.transpose(ak.reshape(b, seq, A_HEADS, 2, HEAD_DIM), (0, 2, 3, 1, 4)), k_norm)
    av = jnp.transpose(av.reshape(b, seq, A_HEADS, 2 * HEAD_DIM), (0, 2, 1, 3))
    lam = jnp.exp(jnp.sum(lam_p[0] * lam_p[1])) - jnp.exp(jnp.sum(lam_p[2] * lam_p[3])) + lam_init
    ya = diff_attention(aq, ak, av, t5_bias, lam)
    ya = rms_norm(ya, subln) * (1.0 - lam_init)
    bq = jnp.transpose(bq.reshape(b, seq, B_HEADS, HEAD_DIM), (0, 2, 1, 3))
    bk = jnp.transpose(bk.reshape(b, seq, B_HEADS, HEAD_DIM), (0, 2, 1, 3))
    bv = jnp.transpose(bv.reshape(b, seq, B_HEADS, HEAD_DIM), (0, 2, 1, 3))
    yb = stick_breaking(bq, bk, bv)
    y = jnp.concatenate([jnp.transpose(ya, (0, 2, 1, 3)).reshape(b, seq, A_WIDTH),
                         jnp.transpose(yb, (0, 2, 1, 3)).reshape(b, seq, B_WIDTH)], axis=-1)
    return y @ w_out


def odd_mixer(u, w_in, q_norm, k_norm, rel_bias, w_out):
    b, seq, _ = u.shape
    q, k, v = jnp.split(u @ w_in, 3, axis=-1)
    q = rms_norm(jnp.transpose(q.reshape(b, seq, C_HEADS, HEAD_DIM), (0, 2, 1, 3)), q_norm)
    k = rms_norm(jnp.transpose(k.reshape(b, seq, C_HEADS, HEAD_DIM), (0, 2, 1, 3)), k_norm)
    v = jnp.transpose(v.reshape(b, seq, C_HEADS, HEAD_DIM), (0, 2, 1, 3))
    y = chunk_band_attention(q, k, v, rel_bias)
    return jnp.transpose(y, (0, 2, 1, 3)).reshape(b, seq, C_WIDTH) @ w_out


def conv_glu_ffn(u, w_up, conv_w, conv_b, w_down):
    seq = u.shape[1]
    hdn = u @ w_up
    hp = jnp.pad(hdn, ((0, 0), (CONV_WIDTH - 1, 0), (0, 0)))
    hc = conv_b
    for j in range(CONV_WIDTH):
        hc = hc + hp[:, j:j + seq] * conv_w[j]
    gate, val = jnp.split(hc, 2, axis=-1)
    return (jax.nn.silu(gate) * val) @ w_down


def setup_inputs(seed: int = 0) -> dict:
    key = jax.random.key(seed)
    ks = jax.random.split(key, 19)
    f32 = jnp.float32
    n = lambda k, shape, s: jax.random.normal(k, shape, f32) * s
    ab_cols = 3 * A_WIDTH + 3 * B_WIDTH
    return {
        'x': n(ks[0], (BATCH, SEQ, D_MODEL), 1.0),
        't5_bias': n(ks[1], (T5_BUCKETS, A_HEADS), 0.5),
        'norm_mix': 1.0 + n(ks[2], (DEPTH, D_MODEL), 0.02),
        'norm_ffn': 1.0 + n(ks[3], (DEPTH, D_MODEL), 0.02),
        'ab_w_in': n(ks[4], (N_EVEN, D_MODEL, ab_cols), D_MODEL ** -0.5),
        'a_q_norm': 1.0 + n(ks[5], (N_EVEN, HEAD_DIM), 0.02),
        'a_k_norm': 1.0 + n(ks[6], (N_EVEN, HEAD_DIM), 0.02),
        'a_lambda': n(ks[7], (N_EVEN, 4, HEAD_DIM), 0.1),
        'a_subln': 1.0 + n(ks[8], (N_EVEN, 2 * HEAD_DIM), 0.02),
        'ab_w_out': n(ks[9], (N_EVEN, A_WIDTH + B_WIDTH, D_MODEL), (A_WIDTH + B_WIDTH) ** -0.5),
        'c_w_in': n(ks[10], (N_ODD, D_MODEL, 3 * C_WIDTH), D_MODEL ** -0.5),
        'c_q_norm': 1.0 + n(ks[11], (N_ODD, HEAD_DIM), 0.02),
        'c_k_norm': 1.0 + n(ks[12], (N_ODD, HEAD_DIM), 0.02),
        'c_rel_bias': n(ks[13], (N_ODD, 2 * C_MAX_REL + 1, C_HEADS), 0.5),
        'c_w_out': n(ks[14], (N_ODD, C_WIDTH, D_MODEL), C_WIDTH ** -0.5),
        'ffn_w_up': n(ks[15], (DEPTH, D_MODEL, 2 * D_FF), D_MODEL ** -0.5),
        'ffn_conv_w': n(ks[16], (DEPTH, CONV_WIDTH, 2 * D_FF), CONV_WIDTH ** -0.5),
        'ffn_conv_b': n(ks[17], (DEPTH, 2 * D_FF), 0.02),
        'ffn_w_down': n(ks[18], (DEPTH, D_FF, D_MODEL), D_FF ** -0.5),
    }


def reference(x, t5_bias, norm_mix, norm_ffn, ab_w_in, a_q_norm, a_k_norm, a_lambda, a_subln, ab_w_out,
              c_w_in, c_q_norm, c_k_norm, c_rel_bias, c_w_out, ffn_w_up, ffn_conv_w, ffn_conv_b, ffn_w_down):
    h = x
    for layer in range(DEPTH):
        u = rms_norm(h, norm_mix[layer])
        if layer % 2 == 0:
            e = layer // 2
            lam_init = 0.8 - 0.6 * math.exp(-0.3 * layer)
            h = h + even_mixer(u, ab_w_in[e], a_q_norm[e], a_k_norm[e], a_lambda[e], a_subln[e],
                               ab_w_out[e], t5_bias, lam_init)
        else:
            o = layer // 2
            h = h + odd_mixer(u, c_w_in[o], c_q_norm[o], c_k_norm[o], c_rel_bias[o], c_w_out[o])
        u = rms_norm(h, norm_ffn[layer])
        h = h + conv_glu_ffn(u, ffn_w_up[layer], ffn_conv_w[layer], ffn_conv_b[layer], ffn_w_down[layer])
    return h
```

```python
import functools
import math

import jax
import jax.numpy as jnp
from jax import lax
from jax.experimental import pallas as pl
from jax.experimental.pallas import tpu as pltpu

D_MODEL = 1024
HEAD_DIM = 64
CHUNK = 64
A_HEADS = 4
B_HEADS = 8
C_HEADS = 16
A_WIDTH = A_HEADS * 2 * HEAD_DIM
B_WIDTH = B_HEADS * HEAD_DIM
C_WIDTH = C_HEADS * HEAD_DIM
T5_BUCKETS = 32
T5_MAX_DIST = 128
C_LEFT_CHUNKS = 8
C_MAX_REL = 128
D_FF = 2816
CONV_WIDTH = 3
EPS = 1e-6

LANES = 128
MXU_COLS = 256
ROW_TILE = 512
ATT_BLOCK = 256
FF_CHUNK = 256
CONV_HALO = 16
VMEM_LIMIT = 56 * 1024 * 1024

F32 = jnp.float32
BF16 = jnp.bfloat16


def _rms(x, g):
    ms = jnp.mean(x * x, axis=-1, keepdims=True)
    return x * lax.rsqrt(ms + EPS) * g


def _lane_lo(shape):
    return lax.broadcasted_iota(jnp.int32, shape, len(shape) - 1) < HEAD_DIM


def _norm_proj_kernel(x_ref, g_ref, w_ref, gain_ref, o_ref, *, norm_chunks):
    u = _rms(x_ref[...], g_ref[...]).astype(BF16)
    tm = u.shape[0]
    lo = _lane_lo((tm, LANES))
    n_cols = o_ref.shape[1]
    for c in range(n_cols // MXU_COLS):
        acc = jnp.dot(u, w_ref[:, c * MXU_COLS:(c + 1) * MXU_COLS], preferred_element_type=F32)
        for half in range(MXU_COLS // LANES):
            col = c * MXU_COLS + half * LANES
            y = acc[:, half * LANES:(half + 1) * LANES]
            if norm_chunks[col // LANES]:
                sq = y * y
                s0 = jnp.sum(jnp.where(lo, sq, 0.0), axis=-1, keepdims=True)
                s1 = jnp.sum(jnp.where(lo, 0.0, sq), axis=-1, keepdims=True)
                ms = jnp.where(lo, s0, s1) * (1.0 / HEAD_DIM)
                y = y * lax.rsqrt(ms + EPS)
            o_ref[:, col:col + LANES] = (y * gain_ref[:, col:col + LANES]).astype(BF16)


def _norm_proj(h2, g, w, gain, norm_chunks):
    t, d = h2.shape
    n = w.shape[1]
    return pl.pallas_call(
        functools.partial(_norm_proj_kernel, norm_chunks=norm_chunks),
        grid=(t // ROW_TILE,),
        in_specs=[
            pl.BlockSpec((ROW_TILE, d), lambda i: (i, 0)),
            pl.BlockSpec((1, d), lambda i: (0, 0)),
            pl.BlockSpec((d, n), lambda i: (0, 0)),
            pl.BlockSpec((1, n), lambda i: (0, 0)),
        ],
        out_specs=pl.BlockSpec((ROW_TILE, n), lambda i: (i, 0)),
        out_shape=jax.ShapeDtypeStruct((t, n), BF16),
        compiler_params=pltpu.CompilerParams(
            dimension_semantics=("parallel",), vmem_limit_bytes=VMEM_LIMIT),
        name="norm_proj",
    )(h2, g.reshape(1, d), w, gain.reshape(1, n))


def _out_proj_kernel(h_ref, *refs):
    y_refs, w_ref, o_ref = refs[:-2], refs[-2], refs[-1]
    acc = h_ref[...]
    row = 0
    for y_ref in y_refs:
        k = y_ref.shape[1]
        acc = acc + jnp.dot(y_ref[...], w_ref[row:row + k, :], preferred_element_type=F32)
        row += k
    o_ref[...] = acc


def _out_proj(h2, ys, w):
    t, d = h2.shape
    in_specs = [pl.BlockSpec((ROW_TILE, d), lambda i: (i, 0))]
    in_specs += [pl.BlockSpec((ROW_TILE, y.shape[1]), lambda i: (i, 0)) for y in ys]
    in_specs += [pl.BlockSpec(w.shape, lambda i: (0, 0))]
    return pl.pallas_call(
        _out_proj_kernel,
        grid=(t // ROW_TILE,),
        in_specs=in_specs,
        out_specs=pl.BlockSpec((ROW_TILE, d), lambda i: (i, 0)),
        out_shape=jax.ShapeDtypeStruct((t, d), F32),
        compiler_params=pltpu.CompilerParams(
            dimension_semantics=("parallel",), vmem_limit_bytes=VMEM_LIMIT),
        name="out_proj",
    )(h2, *ys, w)


def _ffn_kernel(h_ref, halo_ref, g_ref, wup_ref, cw_ref, cb_ref, wdn_ref, o_ref, acc_ref,
                *, tiles_per_seq):
    x = h_ref[...]
    g = g_ref[...]
    tm = x.shape[0]
    u_ext = jnp.concatenate([_rms(halo_ref[...], g), _rms(x, g)], axis=0).astype(BF16)
    keep_halo = (pl.program_id(0) % tiles_per_seq) != 0
    row = lax.broadcasted_iota(jnp.int32, (tm + CONV_HALO, FF_CHUNK), 0)
    live = jnp.logical_or(row >= CONV_HALO, keep_halo)

    def conv(col):
        hd = jnp.dot(u_ext, wup_ref[:, col:col + FF_CHUNK], preferred_element_type=F32)
        hd = jnp.where(live, hd, 0.0)
        out = cb_ref[:, col:col + FF_CHUNK] + hd[CONV_HALO:] * cw_ref[2:3, col:col + FF_CHUNK]
        for back in (1, 2):
            shifted = pltpu.roll(hd, back, 0)[CONV_HALO:]
            out = out + shifted * cw_ref[2 - back:3 - back, col:col + FF_CHUNK]
        return out

    for c in range(D_FF // FF_CHUNK):
        gate = conv(c * FF_CHUNK)
        val = conv(D_FF + c * FF_CHUNK)
        act = (gate * jax.nn.sigmoid(gate) * val).astype(BF16)
        part = jnp.dot(act, wdn_ref[c * FF_CHUNK:(c + 1) * FF_CHUNK, :], preferred_element_type=F32)
        if c == 0:
            acc_ref[...] = x + part
        else:
            acc_ref[...] += part
    o_ref[...] = acc_ref[...]


def _ffn(h2, g, w_up, conv_w, conv_b, w_down, seq):
    t, d = h2.shape
    halo_blocks = ROW_TILE // CONV_HALO
    const = lambda i: (0, 0)
    return pl.pallas_call(
        functools.partial(_ffn_kernel, tiles_per_seq=seq // ROW_TILE),
        grid=(t // ROW_TILE,),
        in_specs=[
            pl.BlockSpec((ROW_TILE, d), lambda i: (i, 0)),
            pl.BlockSpec((CONV_HALO, d), lambda i: (jnp.maximum(i * halo_blocks - 1, 0), 0)),
            pl.BlockSpec((1, d), const),
            pl.BlockSpec(w_up.shape, const, pipeline_mode=pl.Buffered(1)),
            pl.BlockSpec(conv_w.shape, const),
            pl.BlockSpec((1, 2 * D_FF), const),
            pl.BlockSpec(w_down.shape, const, pipeline_mode=pl.Buffered(1)),
        ],
        out_specs=pl.BlockSpec((ROW_TILE, d), lambda i: (i, 0)),
        out_shape=jax.ShapeDtypeStruct((t, d), F32),
        scratch_shapes=[pltpu.VMEM((ROW_TILE, d), F32)],
        compiler_params=pltpu.CompilerParams(
            dimension_semantics=("parallel",), vmem_limit_bytes=VMEM_LIMIT),
        name="conv_glu_ffn",
    )(h2, h2, g.reshape(1, d), w_up, conv_w, conv_b.reshape(1, 2 * D_FF), w_down)


def _split_heads(q):
    lo = _lane_lo(q.shape)
    zero = jnp.zeros_like(q)
    return jnp.concatenate([jnp.where(lo, q, zero), jnp.where(lo, zero, q)], axis=0)


def _qk(qq, k):
    return lax.dot_general(qq, k, (((1,), (1,)), ((), ())), preferred_element_type=F32)


def _tile_ids(shape):
    return (lax.broadcasted_iota(jnp.int32, shape, 0), lax.broadcasted_iota(jnp.int32, shape, 1))


def _diff_attn_kernel(lam_ref, q_ref, k_ref, v_ref, bias_ref, subln_ref, o_ref, m_ref, l_ref, acc_ref,
                      *, out_scale):
    bq = ATT_BLOCK
    qi = pl.program_id(2)
    qq = _split_heads(q_ref[0])
    m_ref[...] = jnp.full(m_ref.shape, -jnp.inf, F32)
    l_ref[...] = jnp.zeros(l_ref.shape, F32)
    acc_ref[...] = jnp.zeros(acc_ref.shape, F32)

    def update(kj, diagonal):
        start = pl.multiple_of(kj * bq, bq)
        k = k_ref[0, pl.ds(start, bq), :]
        v = v_ref[0, pl.ds(start, bq), :]
        bias = bias_ref[0, jnp.minimum(qi - kj, 2)]
        s = _qk(qq, k) + jnp.concatenate([bias, bias], axis=0)
        if diagonal:
            r, c = _tile_ids((bq, bq))
            mask = (c // CHUNK) <= (r // CHUNK)
            s = jnp.where(jnp.concatenate([mask, mask], axis=0), s, -jnp.inf)
        m_prev = m_ref[...]
        m_new = jnp.maximum(m_prev, jnp.max(s, axis=1, keepdims=True))
        alpha = jnp.exp(m_prev - m_new)
        p = jnp.exp(s - m_new)
        l_ref[...] = alpha * l_ref[...] + jnp.sum(p, axis=1, keepdims=True)
        acc_ref[...] = alpha * acc_ref[...] + jnp.dot(p.astype(BF16), v, preferred_element_type=F32)
        m_ref[...] = m_new

    def body(kj, carry):
        update(kj, False)
        return carry

    lax.fori_loop(0, qi, body, 0)
    update(qi, True)

    o = acc_ref[...] / l_ref[...]
    ya = o[:bq] - lam_ref[0, 0] * o[bq:]
    ya = _rms(ya, subln_ref[...]) * out_scale
    o_ref[0] = ya.astype(BF16)


def _diff_attn(proj, bias_tab, subln, lam, out_scale):
    b, s, _ = proj.shape
    nq = s // ATT_BLOCK
    q_blk, k_blk, v_blk = 0, A_WIDTH // LANES, 2 * A_WIDTH // LANES
    return pl.pallas_call(
        functools.partial(_diff_attn_kernel, out_scale=out_scale),
        grid=(b, A_HEADS, nq),
        in_specs=[
            pl.BlockSpec(memory_space=pltpu.SMEM),
            pl.BlockSpec((1, ATT_BLOCK, LANES), lambda bi, h, qi: (bi, qi, q_blk + h)),
            pl.BlockSpec((1, s, LANES), lambda bi, h, qi: (bi, 0, k_blk + h)),
            pl.BlockSpec((1, s, LANES), lambda bi, h, qi: (bi, 0, v_blk + h)),
            pl.BlockSpec((1, 3, ATT_BLOCK, ATT_BLOCK), lambda bi, h, qi: (h, 0, 0, 0)),
            pl.BlockSpec((1, LANES), lambda bi, h, qi: (0, 0)),
        ],
        out_specs=pl.BlockSpec((1, ATT_BLOCK, LANES), lambda bi, h, qi: (bi, qi, h)),
        out_shape=jax.ShapeDtypeStruct((b, s, A_WIDTH), BF16),
        scratch_shapes=[
            pltpu.VMEM((2 * ATT_BLOCK, 1), F32),
            pltpu.VMEM((2 * ATT_BLOCK, 1), F32),
            pltpu.VMEM((2 * ATT_BLOCK, LANES), F32),
        ],
        compiler_params=pltpu.CompilerParams(
            dimension_semantics=("parallel", "parallel", "arbitrary"), vmem_limit_bytes=VMEM_LIMIT),
        name="diff_attn",
    )(lam.reshape(1, 1), proj, proj, proj, bias_tab, subln.reshape(1, LANES))


def _stick_kernel(q_ref, k_ref, v_ref, o_ref, carry_ref, acc_ref):
    bq = ATT_BLOCK
    qi = pl.program_id(2)
    qq = _split_heads(q_ref[0])
    carry_ref[...] = jnp.zeros(carry_ref.shape, F32)
    acc_ref[...] = jnp.zeros(acc_ref.shape, F32)
    r, c = _tile_ids((bq, bq))
    later = (r > c).astype(BF16)

    def update(kj, diagonal):
        start = pl.multiple_of(kj * bq, bq)
        k = k_ref[0, pl.ds(start, bq), :]
        v = v_ref[0, pl.ds(start, bq), :]
        z = _qk(qq, k)
        log_1m = -(jnp.maximum(z, 0.0) + jnp.log1p(jnp.exp(-jnp.abs(z))))
        if diagonal:
            mask = jnp.concatenate([c < r, c < r], axis=0)
            log_1m = jnp.where(mask, log_1m, 0.0)
        hi = log_1m.astype(BF16)
        lo = (log_1m - hi.astype(F32)).astype(BF16)
        remain = (jnp.dot(hi, later, preferred_element_type=F32)
                  + jnp.dot(lo, later, preferred_element_type=F32))
        carry = carry_ref[...]
        a = jnp.exp(z + log_1m + remain + carry)
        if diagonal:
            a = jnp.where(mask, a, 0.0)
        acc_ref[...] += jnp.dot(a.astype(BF16), v, preferred_element_type=F32)
        carry_ref[...] = carry + jnp.sum(log_1m, axis=1, keepdims=True)

    update(qi, True)

    def body(j, carry):
        update(qi - 1 - j, False)
        return carry

    lax.fori_loop(0, qi, body, 0)

    acc = acc_ref[...]
    o_ref[0] = jnp.where(_lane_lo((bq, LANES)), acc[:bq], acc[bq:]).astype(BF16)


def _stick_attn(proj):
    b, s, _ = proj.shape
    nq = s // ATT_BLOCK
    base = 3 * A_WIDTH // LANES
    q_blk, k_blk, v_blk = base, base + B_WIDTH // LANES, base + 2 * B_WIDTH // LANES
    return pl.pallas_call(
        _stick_kernel,
        grid=(b, B_HEADS // 2, nq),
        in_specs=[
            pl.BlockSpec((1, ATT_BLOCK, LANES), lambda bi, h, qi: (bi, qi, q_blk + h)),
            pl.BlockSpec((1, s, LANES), lambda bi, h, qi: (bi, 0, k_blk + h)),
            pl.BlockSpec((1, s, LANES), lambda bi, h, qi: (bi, 0, v_blk + h)),
        ],
        out_specs=pl.BlockSpec((1, ATT_BLOCK, LANES), lambda bi, h, qi: (bi, qi, h)),
        out_shape=jax.ShapeDtypeStruct((b, s, B_WIDTH), BF16),
        scratch_shapes=[
            pltpu.VMEM((2 * ATT_BLOCK, 1), F32),
            pltpu.VMEM((2 * ATT_BLOCK, LANES), F32),
        ],
        compiler_params=pltpu.CompilerParams(
            dimension_semantics=("parallel", "parallel", "arbitrary"), vmem_limit_bytes=VMEM_LIMIT),
        name="stick_breaking",
    )(proj, proj, proj)


def _band_kernel(q_ref, k_ref, v_ref, bias_ref, o_ref):
    bq = ATT_BLOCK
    qi = pl.program_id(2)
    qq = _split_heads(q_ref[0])
    r, c = _tile_ids((bq, bq))
    rc, cc = r // CHUNK, c // CHUNK
    masks = (cc <= rc, None, cc >= rc)
    scores, vs = [], []
    for d in range(3):
        kj = jnp.maximum(qi - d, 0)
        start = pl.multiple_of(kj * bq, bq)
        k = k_ref[0, pl.ds(start, bq), :]
        vs.append(v_ref[0, pl.ds(start, bq), :])
        bias = jnp.concatenate([bias_ref[0, d], bias_ref[1, d]], axis=0)
        s = _qk(qq, k) + bias
        if d > 0:
            s = s + jnp.where(qi >= d, 0.0, -jnp.inf)
        if masks[d] is not None:
            s = jnp.where(jnp.concatenate([masks[d], masks[d]], axis=0), s, -jnp.inf)
        scores.append(s)
    m = functools.reduce(jnp.maximum, [jnp.max(s, axis=1, keepdims=True) for s in scores])
    l = jnp.zeros_like(m)
    acc = jnp.zeros((2 * bq, LANES), F32)
    for s, v in zip(scores, vs):
        p = jnp.exp(s - m)
        l = l + jnp.sum(p, axis=1, keepdims=True)
        acc = acc + jnp.dot(p.astype(BF16), v, preferred_element_type=F32)
    o = acc / l
    o_ref[0] = jnp.where(_lane_lo((bq, LANES)), o[:bq], o[bq:]).astype(BF16)


def _band_attn(proj, bias_tab):
    b, s, _ = proj.shape
    nq = s // ATT_BLOCK
    q_blk, k_blk, v_blk = 0, C_WIDTH // LANES, 2 * C_WIDTH // LANES
    return pl.pallas_call(
        _band_kernel,
        grid=(b, C_HEADS // 2, nq),
        in_specs=[
            pl.BlockSpec((1, ATT_BLOCK, LANES), lambda bi, h, qi: (bi, qi, q_blk + h)),
            pl.BlockSpec((1, s, LANES), lambda bi, h, qi: (bi, 0, k_blk + h)),
            pl.BlockSpec((1, s, LANES), lambda bi, h, qi: (bi, 0, v_blk + h)),
            pl.BlockSpec((2, 3, ATT_BLOCK, ATT_BLOCK), lambda bi, h, qi: (h, 0, 0, 0)),
        ],
        out_specs=pl.BlockSpec((1, ATT_BLOCK, LANES), lambda bi, h, qi: (bi, qi, h)),
        out_shape=jax.ShapeDtypeStruct((b, s, C_WIDTH), BF16),
        compiler_params=pltpu.CompilerParams(
            dimension_semantics=("parallel", "parallel", "arbitrary"), vmem_limit_bytes=VMEM_LIMIT),
        name="chunk_band",
    )(proj, proj, proj, bias_tab)


def _t5_bucket(rel):
    nb = T5_BUCKETS // 2
    max_exact = nb // 2
    n = jnp.abs(rel)
    large = max_exact + (jnp.log(jnp.maximum(n, 1).astype(jnp.float32) / max_exact)
                         / math.log(T5_MAX_DIST / max_exact) * (nb - max_exact)).astype(jnp.int32)
    large = jnp.minimum(large, nb - 1)
    return jnp.where(rel > 0, nb, 0) + jnp.where(n < max_exact, n, large)


def _tile_rel():
    i = jnp.arange(ATT_BLOCK)
    d = jnp.arange(3)
    return d[:, None, None] * ATT_BLOCK + i[None, :, None] - i[None, None, :]


def _t5_bias_tiles(t5_bias):
    tab = t5_bias[_t5_bucket(-_tile_rel())]
    return jnp.transpose(tab, (3, 0, 1, 2)).astype(F32)


def _band_bias_tiles(rel_bias):
    rel = jnp.clip(_tile_rel(), -C_MAX_REL, C_MAX_REL) + C_MAX_REL
    return jnp.transpose(rel_bias[rel], (3, 0, 1, 2)).astype(F32)


def _chunk_flags(*sections):
    flags = []
    for width, normed in sections:
        flags += [normed] * (width // LANES)
    return tuple(flags)


def kernel(x, t5_bias, norm_mix, norm_ffn, ab_w_in, a_q_norm, a_k_norm, a_lambda, a_subln, ab_w_out,
           c_w_in, c_q_norm, c_k_norm, c_rel_bias, c_w_out, ffn_w_up, ffn_conv_w, ffn_conv_b, ffn_w_down):
    b, seq, d = x.shape
    depth = norm_mix.shape[0]
    assert d == D_MODEL and seq % ROW_TILE == 0 and seq % ATT_BLOCK == 0
    scale = HEAD_DIM ** -0.5
    h = x.reshape(b * seq, d)
    t5_tiles = _t5_bias_tiles(t5_bias)
    ones = lambda n: jnp.ones((n,), F32)
    even_flags = _chunk_flags((2 * A_WIDTH, True), (A_WIDTH + 3 * B_WIDTH, False))
    odd_flags = _chunk_flags((2 * C_WIDTH, True), (C_WIDTH, False))
    for layer in range(depth):
        if layer % 2 == 0:
            e = layer // 2
            lam_init = 0.8 - 0.6 * math.exp(-0.3 * layer)
            gain = jnp.concatenate([
                jnp.tile(a_q_norm[e], 2 * A_HEADS) * scale, jnp.tile(a_k_norm[e], 2 * A_HEADS), ones(A_WIDTH),
                ones(B_WIDTH) * scale, ones(2 * B_WIDTH)])
            proj = _norm_proj(h, norm_mix[layer], ab_w_in[e].astype(BF16), gain, even_flags)
            proj = proj.reshape(b, seq, -1)
            lam_p = a_lambda[e]
            lam = jnp.exp(jnp.sum(lam_p[0] * lam_p[1])) - jnp.exp(jnp.sum(lam_p[2] * lam_p[3])) + lam_init
            ya = _diff_attn(proj, t5_tiles, a_subln[e], lam.astype(F32), 1.0 - lam_init)
            yb = _stick_attn(proj)
            ys = [ya.reshape(b * seq, A_WIDTH), yb.reshape(b * seq, B_WIDTH)]
            h = _out_proj(h, ys, ab_w_out[e].astype(BF16))
        else:
            o = layer // 2
            gain = jnp.concatenate([
                jnp.tile(c_q_norm[o], C_HEADS) * scale, jnp.tile(c_k_norm[o], C_HEADS), ones(C_WIDTH)])
            proj = _norm_proj(h, norm_mix[layer], c_w_in[o].astype(BF16), gain, odd_flags)
            proj = proj.reshape(b, seq, -1)
            y = _band_attn(proj, _band_bias_tiles(c_rel_bias[o]))
            h = _out_proj(h, [y.reshape(b * seq, C_WIDTH)], c_w_out[o].astype(BF16))
        h = _ffn(h, norm_ffn[layer], ffn_w_up[layer].astype(BF16), ffn_conv_w[layer], ffn_conv_b[layer],
                 ffn_w_down[layer].astype(BF16), seq)
    return h.reshape(b, seq, d)
```

```python
import functools
import math

import jax
import jax.numpy as jnp
from jax import lax
from jax.experimental import pallas as pl
from jax.experimental.pallas import tpu as pltpu

D_MODEL = 1024
HEAD_DIM = 64
CHUNK = 64
A_HEADS = 4
B_HEADS = 8
C_HEADS = 16
A_WIDTH = A_HEADS * 2 * HEAD_DIM
B_WIDTH = B_HEADS * HEAD_DIM
C_WIDTH = C_HEADS * HEAD_DIM
T5_BUCKETS = 32
T5_MAX_DIST = 128
C_LEFT_CHUNKS = 8
C_MAX_REL = 128
D_FF = 2816
CONV_WIDTH = 3
EPS = 1e-6

LANES = 128
MXU_COLS = 256
ROW_TILE = 512
ATT_BLOCK = 256
FF_CHUNK = 256
CONV_HALO = 16
FAR_GROUP = 2
VMEM_LIMIT = 56 * 1024 * 1024
LOG2_E = math.log2(math.e)

F32 = jnp.float32
BF16 = jnp.bfloat16


def _rms(x, g):
    ms = jnp.mean(x * x, axis=-1, keepdims=True)
    return x * lax.rsqrt(ms + EPS) * g


def _lane_lo(shape):
    return lax.broadcasted_iota(jnp.int32, shape, len(shape) - 1) < HEAD_DIM


def _norm_proj_kernel(x_ref, g_ref, w_ref, gain_ref, o_ref, *, norm_chunks):
    u = _rms(x_ref[...], g_ref[...]).astype(BF16)
    tm = u.shape[0]
    lo = _lane_lo((tm, LANES))
    n_cols = o_ref.shape[1]
    for c in range(n_cols // MXU_COLS):
        acc = jnp.dot(u, w_ref[:, c * MXU_COLS:(c + 1) * MXU_COLS], preferred_element_type=F32)
        for half in range(MXU_COLS // LANES):
            col = c * MXU_COLS + half * LANES
            y = acc[:, half * LANES:(half + 1) * LANES]
            if norm_chunks[col // LANES]:
                sq = y * y
                s0 = jnp.sum(jnp.where(lo, sq, 0.0), axis=-1, keepdims=True)
                s1 = jnp.sum(jnp.where(lo, 0.0, sq), axis=-1, keepdims=True)
                ms = jnp.where(lo, s0, s1) * (1.0 / HEAD_DIM)
                y = y * lax.rsqrt(ms + EPS)
            o_ref[:, col:col + LANES] = (y * gain_ref[:, col:col + LANES]).astype(BF16)


def _norm_proj(h2, g, w, gain, norm_chunks):
    t, d = h2.shape
    n = w.shape[1]
    return pl.pallas_call(
        functools.partial(_norm_proj_kernel, norm_chunks=norm_chunks),
        grid=(t // ROW_TILE,),
        in_specs=[
            pl.BlockSpec((ROW_TILE, d), lambda i: (i, 0)),
            pl.BlockSpec((1, d), lambda i: (0, 0)),
            pl.BlockSpec((d, n), lambda i: (0, 0)),
            pl.BlockSpec((1, n), lambda i: (0, 0)),
        ],
        out_specs=pl.BlockSpec((ROW_TILE, n), lambda i: (i, 0)),
        out_shape=jax.ShapeDtypeStruct((t, n), BF16),
        compiler_params=pltpu.CompilerParams(
            dimension_semantics=("parallel",), vmem_limit_bytes=VMEM_LIMIT),
        name="norm_proj",
    )(h2, g.reshape(1, d), w, gain.reshape(1, n))


def _out_proj_kernel(h_ref, *refs):
    y_refs, w_ref, o_ref = refs[:-2], refs[-2], refs[-1]
    acc = h_ref[...]
    row = 0
    for y_ref in y_refs:
        k = y_ref.shape[1]
        acc = acc + jnp.dot(y_ref[...], w_ref[row:row + k, :], preferred_element_type=F32)
        row += k
    o_ref[...] = acc


def _out_proj(h2, ys, w):
    t, d = h2.shape
    in_specs = [pl.BlockSpec((ROW_TILE, d), lambda i: (i, 0))]
    in_specs += [pl.BlockSpec((ROW_TILE, y.shape[1]), lambda i: (i, 0)) for y in ys]
    in_specs += [pl.BlockSpec(w.shape, lambda i: (0, 0))]
    return pl.pallas_call(
        _out_proj_kernel,
        grid=(t // ROW_TILE,),
        in_specs=in_specs,
        out_specs=pl.BlockSpec((ROW_TILE, d), lambda i: (i, 0)),
        out_shape=jax.ShapeDtypeStruct((t, d), F32),
        compiler_params=pltpu.CompilerParams(
            dimension_semantics=("parallel",), vmem_limit_bytes=VMEM_LIMIT),
        name="out_proj",
    )(h2, *ys, w)


def _ffn_kernel(h_ref, halo_ref, g_ref, wup_ref, cw_ref, cb_ref, wdn_ref, o_ref, acc_ref,
                *, tiles_per_seq):
    x = h_ref[...]
    g = g_ref[...]
    tm = x.shape[0]
    u_ext = jnp.concatenate([_rms(halo_ref[...], g), _rms(x, g)], axis=0).astype(BF16)
    keep_halo = (pl.program_id(0) % tiles_per_seq) != 0
    row = lax.broadcasted_iota(jnp.int32, (tm + CONV_HALO, FF_CHUNK), 0)
    live = jnp.logical_or(row >= CONV_HALO, keep_halo)

    def conv(col):
        hd = jnp.dot(u_ext, wup_ref[:, col:col + FF_CHUNK], preferred_element_type=F32)
        hd = jnp.where(live, hd, 0.0)
        out = cb_ref[:, col:col + FF_CHUNK] + hd[CONV_HALO:] * cw_ref[2:3, col:col + FF_CHUNK]
        for back in (1, 2):
            shifted = pltpu.roll(hd, back, 0)[CONV_HALO:]
            out = out + shifted * cw_ref[2 - back:3 - back, col:col + FF_CHUNK]
        return out

    for c in range(D_FF // FF_CHUNK):
        gate = conv(c * FF_CHUNK)
        val = conv(D_FF + c * FF_CHUNK)
        act = (gate * jax.nn.sigmoid(gate) * val).astype(BF16)
        part = jnp.dot(act, wdn_ref[c * FF_CHUNK:(c + 1) * FF_CHUNK, :], preferred_element_type=F32)
        if c == 0:
            acc_ref[...] = x + part
        else:
            acc_ref[...] += part
    o_ref[...] = acc_ref[...]


def _ffn(h2, g, w_up, conv_w, conv_b, w_down, seq):
    t, d = h2.shape
    halo_blocks = ROW_TILE // CONV_HALO
    const = lambda i: (0, 0)
    return pl.pallas_call(
        functools.partial(_ffn_kernel, tiles_per_seq=seq // ROW_TILE),
        grid=(t // ROW_TILE,),
        in_specs=[
            pl.BlockSpec((ROW_TILE, d), lambda i: (i, 0)),
            pl.BlockSpec((CONV_HALO, d), lambda i: (jnp.maximum(i * halo_blocks - 1, 0), 0)),
            pl.BlockSpec((1, d), const),
            pl.BlockSpec(w_up.shape, const, pipeline_mode=pl.Buffered(1)),
            pl.BlockSpec(conv_w.shape, const),
            pl.BlockSpec((1, 2 * D_FF), const),
            pl.BlockSpec(w_down.shape, const, pipeline_mode=pl.Buffered(1)),
        ],
        out_specs=pl.BlockSpec((ROW_TILE, d), lambda i: (i, 0)),
        out_shape=jax.ShapeDtypeStruct((t, d), F32),
        scratch_shapes=[pltpu.VMEM((ROW_TILE, d), F32)],
        compiler_params=pltpu.CompilerParams(
            dimension_semantics=("parallel",), vmem_limit_bytes=VMEM_LIMIT),
        name="conv_glu_ffn",
    )(h2, h2, g.reshape(1, d), w_up, conv_w, conv_b.reshape(1, 2 * D_FF), w_down)


def _split_heads(q):
    lo = _lane_lo(q.shape)
    zero = jnp.zeros_like(q)
    return jnp.concatenate([jnp.where(lo, q, zero), jnp.where(lo, zero, q)], axis=0)


def _nt_dot(a, b):
    return lax.dot_general(a, b, (((1,), (1,)), ((), ())), preferred_element_type=F32)


def _tile_ids(shape):
    return (lax.broadcasted_iota(jnp.int32, shape, 0), lax.broadcasted_iota(jnp.int32, shape, 1))


def _v_tiles(v):
    b, s, w = v.shape
    v = v.reshape(b, s // ATT_BLOCK, ATT_BLOCK, w // LANES, LANES)
    return jnp.transpose(v, (0, 3, 1, 4, 2))


def _diff_attn_kernel(lam_ref, q_ref, k_ref, vt_ref, bias_ref, subln_ref, o_ref, m_ref, l_ref, acc_ref,
                      *, out_scale):
    blk = ATT_BLOCK
    qi = pl.program_id(2)
    qq = _split_heads(q_ref[0])
    m_ref[...] = jnp.full(m_ref.shape, -jnp.inf, F32)
    l_ref[...] = jnp.zeros(l_ref.shape, F32)
    acc_ref[...] = jnp.zeros(acc_ref.shape, F32)

    def scores(kj):
        start = pl.multiple_of(kj * blk, blk)
        return _nt_dot(k_ref[0, pl.ds(start, blk), :], qq)

    def absorb(s, kj):
        m_prev = m_ref[...]
        m_new = jnp.maximum(m_prev, jnp.max(s, axis=0, keepdims=True))
        alpha = jnp.exp2(m_prev - m_new)
        p = jnp.exp2(s - m_new)
        l_ref[...] = alpha * l_ref[...] + jnp.sum(p, axis=0, keepdims=True)
        pv = jnp.dot(vt_ref[0, 0, kj], p.astype(BF16), preferred_element_type=F32)
        acc_ref[...] = alpha * acc_ref[...] + pv
        m_ref[...] = m_new

    both = lambda t: jnp.concatenate([t, t], axis=1)
    kk, qc = _tile_ids((blk, blk))
    own_chunks = (kk >> 6) <= (qc >> 6)
    prev = jnp.maximum(qi - 1, 0)
    s_own = scores(qi) + both(bias_ref[0, 0])
    s_prev = scores(prev) + both(bias_ref[0, 1] + jnp.where(qi >= 1, 0.0, -jnp.inf))
    absorb(jnp.where(both(own_chunks), s_own, -jnp.inf), qi)
    absorb(s_prev, prev)

    n_far = jnp.maximum(qi - 1, 0)

    def far_group(g, carry):
        tiles = [qi - 2 - g * FAR_GROUP - u for u in range(FAR_GROUP)]
        for s, t in zip([scores(t) for t in tiles], tiles):
            absorb(s, t)
        return carry

    def far_single(t, carry):
        absorb(scores(t), t)
        return carry

    lax.fori_loop(0, n_far // FAR_GROUP, far_group, 0)
    lax.fori_loop(0, n_far % FAR_GROUP, far_single, 0)

    o = acc_ref[...] / l_ref[...]
    ya = o[:, :blk] - lam_ref[0, 0] * o[:, blk:]
    ms = jnp.mean(ya * ya, axis=0, keepdims=True)
    ya = ya * lax.rsqrt(ms + EPS) * (subln_ref[...] * out_scale)
    o_ref[0] = ya.T.astype(BF16)


def _diff_attn(proj, bias_tab, subln, lam, out_scale):
    b, s, _ = proj.shape
    nq = s // ATT_BLOCK
    q_blk, k_blk = 0, A_WIDTH // LANES
    vt = _v_tiles(proj[:, :, 2 * A_WIDTH:3 * A_WIDTH])
    return pl.pallas_call(
        functools.partial(_diff_attn_kernel, out_scale=out_scale),
        grid=(b, A_HEADS, nq),
        in_specs=[
            pl.BlockSpec(memory_space=pltpu.SMEM),
            pl.BlockSpec((1, ATT_BLOCK, LANES), lambda bi, h, qi: (bi, qi, q_blk + h)),
            pl.BlockSpec((1, s, LANES), lambda bi, h, qi: (bi, 0, k_blk + h)),
            pl.BlockSpec((1, 1, nq, LANES, ATT_BLOCK), lambda bi, h, qi: (bi, h, 0, 0, 0)),
            pl.BlockSpec((1, 2, ATT_BLOCK, ATT_BLOCK), lambda bi, h, qi: (h, 0, 0, 0)),
            pl.BlockSpec((LANES, 1), lambda bi, h, qi: (0, 0)),
        ],
        out_specs=pl.BlockSpec((1, ATT_BLOCK, LANES), lambda bi, h, qi: (bi, qi, h)),
        out_shape=jax.ShapeDtypeStruct((b, s, A_WIDTH), BF16),
        scratch_shapes=[
            pltpu.VMEM((1, 2 * ATT_BLOCK), F32),
            pltpu.VMEM((1, 2 * ATT_BLOCK), F32),
            pltpu.VMEM((LANES, 2 * ATT_BLOCK), F32),
        ],
        compiler_params=pltpu.CompilerParams(
            dimension_semantics=("parallel", "parallel", "arbitrary"), vmem_limit_bytes=VMEM_LIMIT),
        name="diff_attn",
    )(lam.reshape(1, 1), proj, proj, vt, bias_tab, subln.reshape(LANES, 1))


STICK_DEAD_LOGIT = -104.0


def _stick_kernel(q_ref, k_ref, v_ref, o_ref, carry_ref, acc_ref):
    bq = ATT_BLOCK
    qi = pl.program_id(2)
    qq = _split_heads(q_ref[0])
    carry_ref[...] = jnp.zeros(carry_ref.shape, F32)
    acc_ref[...] = jnp.zeros(acc_ref.shape, F32)
    r, c = _tile_ids((bq, bq))
    later = jnp.where(r > c, 1.0, 0.0).astype(BF16)

    def update(kj, diagonal):
        start = pl.multiple_of(kj * bq, bq)
        k = k_ref[0, pl.ds(start, bq), :]
        v = v_ref[0, pl.ds(start, bq), :]
        z = _nt_dot(qq, k)
        log_1m = -(jnp.maximum(z, 0.0) + jnp.log1p(jnp.exp(-jnp.abs(z))))
        if diagonal:
            mask = jnp.concatenate([c < r, c < r], axis=0)
            log_1m = jnp.where(mask, log_1m, 0.0)
        hi = log_1m.astype(BF16)
        lo = (log_1m - hi.astype(F32)).astype(BF16)
        remain = (jnp.dot(hi, later, preferred_element_type=F32)
                  + jnp.dot(lo, later, preferred_element_type=F32))
        carry = carry_ref[...]
        a = jnp.exp(z + log_1m + remain + carry)
        if diagonal:
            a = jnp.where(mask, a, 0.0)
        acc_ref[...] += jnp.dot(a.astype(BF16), v, preferred_element_type=F32)
        carry_ref[...] = carry + jnp.sum(log_1m, axis=1, keepdims=True)

    def live():
        return jnp.max(carry_ref[...]) > STICK_DEAD_LOGIT

    update(qi, True)

    def body(state):
        j, _ = state
        update(qi - 1 - j, False)
        return j + 1, live()

    lax.while_loop(lambda st: jnp.logical_and(st[0] < qi, st[1]), body, (jnp.int32(0), live()))

    acc = acc_ref[...]
    o_ref[0] = jnp.where(_lane_lo((bq, LANES)), acc[:bq], acc[bq:]).astype(BF16)


def _stick_attn(proj):
    b, s, _ = proj.shape
    nq = s // ATT_BLOCK
    base = 3 * A_WIDTH // LANES
    q_blk, k_blk, v_blk = base, base + B_WIDTH // LANES, base + 2 * B_WIDTH // LANES
    return pl.pallas_call(
        _stick_kernel,
        grid=(b, B_HEADS // 2, nq),
        in_specs=[
            pl.BlockSpec((1, ATT_BLOCK, LANES), lambda bi, h, qi: (bi, qi, q_blk + h)),
            pl.BlockSpec((1, s, LANES), lambda bi, h, qi: (bi, 0, k_blk + h)),
            pl.BlockSpec((1, s, LANES), lambda bi, h, qi: (bi, 0, v_blk + h)),
        ],
        out_specs=pl.BlockSpec((1, ATT_BLOCK, LANES), lambda bi, h, qi: (bi, qi, h)),
        out_shape=jax.ShapeDtypeStruct((b, s, B_WIDTH), BF16),
        scratch_shapes=[
            pltpu.VMEM((2 * ATT_BLOCK, 1), F32),
            pltpu.VMEM((2 * ATT_BLOCK, LANES), F32),
        ],
        compiler_params=pltpu.CompilerParams(
            dimension_semantics=("parallel", "parallel", "arbitrary"), vmem_limit_bytes=VMEM_LIMIT),
        name="stick_breaking",
    )(proj, proj, proj)


def _band_kernel(q_ref, k_ref, vt_ref, bias_ref, o_ref):
    blk = ATT_BLOCK
    qi = pl.program_id(2)
    qq = _split_heads(q_ref[0])
    kk, qc = _tile_ids((blk, blk))
    kc, qc = kk // CHUNK, qc // CHUNK
    masks = (kc <= qc, None, kc >= qc)
    scores, vts = [], []
    for d in range(3):
        kj = jnp.maximum(qi - d, 0)
        start = pl.multiple_of(kj * blk, blk)
        k = k_ref[0, pl.ds(start, blk), :]
        vts.append(vt_ref[0, 0, kj])
        s = _nt_dot(k, qq) + jnp.concatenate([bias_ref[0, d], bias_ref[1, d]], axis=1)
        if d > 0:
            s = s + jnp.where(qi >= d, 0.0, -jnp.inf)
        if masks[d] is not None:
            s = jnp.where(jnp.concatenate([masks[d], masks[d]], axis=1), s, -jnp.inf)
        scores.append(s)
    m = functools.reduce(jnp.maximum, [jnp.max(s, axis=0, keepdims=True) for s in scores])
    l = jnp.zeros_like(m)
    acc = jnp.zeros((LANES, 2 * blk), F32)
    for s, vt in zip(scores, vts):
        p = jnp.exp(s - m)
        l = l + jnp.sum(p, axis=0, keepdims=True)
        acc = acc + jnp.dot(vt, p.astype(BF16), preferred_element_type=F32)
    o = acc / l
    o = jnp.concatenate([o[:HEAD_DIM, :blk], o[HEAD_DIM:, blk:]], axis=0)
    o_ref[0] = o.T.astype(BF16)


def _band_attn(proj, bias_tab):
    b, s, _ = proj.shape
    nq = s // ATT_BLOCK
    q_blk, k_blk = 0, C_WIDTH // LANES
    vt = _v_tiles(proj[:, :, 2 * C_WIDTH:3 * C_WIDTH])
    return pl.pallas_call(
        _band_kernel,
        grid=(b, C_HEADS // 2, nq),
        in_specs=[
            pl.BlockSpec((1, ATT_BLOCK, LANES), lambda bi, h, qi: (bi, qi, q_blk + h)),
            pl.BlockSpec((1, s, LANES), lambda bi, h, qi: (bi, 0, k_blk + h)),
            pl.BlockSpec((1, 1, nq, LANES, ATT_BLOCK), lambda bi, h, qi: (bi, h, 0, 0, 0)),
            pl.BlockSpec((2, 3, ATT_BLOCK, ATT_BLOCK), lambda bi, h, qi: (h, 0, 0, 0)),
        ],
        out_specs=pl.BlockSpec((1, ATT_BLOCK, LANES), lambda bi, h, qi: (bi, qi, h)),
        out_shape=jax.ShapeDtypeStruct((b, s, C_WIDTH), BF16),
        compiler_params=pltpu.CompilerParams(
            dimension_semantics=("parallel", "parallel", "arbitrary"), vmem_limit_bytes=VMEM_LIMIT),
        name="chunk_band",
    )(proj, proj, vt, bias_tab)


def _t5_bucket(rel):
    nb = T5_BUCKETS // 2
    max_exact = nb // 2
    n = jnp.abs(rel)
    large = max_exact + (jnp.log(jnp.maximum(n, 1).astype(jnp.float32) / max_exact)
                         / math.log(T5_MAX_DIST / max_exact) * (nb - max_exact)).astype(jnp.int32)
    large = jnp.minimum(large, nb - 1)
    return jnp.where(rel > 0, nb, 0) + jnp.where(n < max_exact, n, large)


def _toeplitz_tiles(per_distance):
    blk = ATT_BLOCK
    vec = per_distance.T.astype(F32)
    h, period = vec.shape
    skew = jnp.tile(vec, (1, blk))[:, :blk * (period - 1)].reshape(h, blk, period - 1)
    return jnp.stack([skew[:, :, blk - 1 + blk * d:2 * blk - 1 + blk * d] for d in range(3)], axis=1)


def _tile_distances():
    return jnp.arange(4 * ATT_BLOCK) - (ATT_BLOCK - 1)


def _t5_bias_tiles(t5_bias):
    assert ATT_BLOCK >= T5_MAX_DIST
    tiles = _toeplitz_tiles(t5_bias[_t5_bucket(-_tile_distances())])
    far = tiles[:, 2, :1, :1]
    return (tiles[:, :2] - far[:, None]) * LOG2_E


def _band_bias_tiles(rel_bias):
    return _toeplitz_tiles(rel_bias[jnp.clip(_tile_distances(), -C_MAX_REL, C_MAX_REL) + C_MAX_REL])


def _chunk_flags(*sections):
    flags = []
    for width, normed in sections:
        flags += [normed] * (width // LANES)
    return tuple(flags)


def kernel(x, t5_bias, norm_mix, norm_ffn, ab_w_in, a_q_norm, a_k_norm, a_lambda, a_subln, ab_w_out,
           c_w_in, c_q_norm, c_k_norm, c_rel_bias, c_w_out, ffn_w_up, ffn_conv_w, ffn_conv_b, ffn_w_down):
    b, seq, d = x.shape
    depth = norm_mix.shape[0]
    assert d == D_MODEL and seq % ROW_TILE == 0 and seq % ATT_BLOCK == 0
    scale = HEAD_DIM ** -0.5
    h = x.reshape(b * seq, d)
    t5_tiles = _t5_bias_tiles(t5_bias)
    ones = lambda n: jnp.ones((n,), F32)
    even_flags = _chunk_flags((2 * A_WIDTH, True), (A_WIDTH + 3 * B_WIDTH, False))
    odd_flags = _chunk_flags((2 * C_WIDTH, True), (C_WIDTH, False))
    for layer in range(depth):
        if layer % 2 == 0:
            e = layer // 2
            lam_init = 0.8 - 0.6 * math.exp(-0.3 * layer)
            gain = jnp.concatenate([
                jnp.tile(a_q_norm[e], 2 * A_HEADS) * (scale * LOG2_E), jnp.tile(a_k_norm[e], 2 * A_HEADS), ones(A_WIDTH),
                ones(B_WIDTH) * scale, ones(2 * B_WIDTH)])
            proj = _norm_proj(h, norm_mix[layer], ab_w_in[e].astype(BF16), gain, even_flags)
            proj = proj.reshape(b, seq, -1)
            lam_p = a_lambda[e]
            lam = jnp.exp(jnp.sum(lam_p[0] * lam_p[1])) - jnp.exp(jnp.sum(lam_p[2] * lam_p[3])) + lam_init
            ya = _diff_attn(proj, t5_tiles, a_subln[e], lam.astype(F32), 1.0 - lam_init)
            yb = _stick_attn(proj)
            ys = [ya.reshape(b * seq, A_WIDTH), yb.reshape(b * seq, B_WIDTH)]
            h = _out_proj(h, ys, ab_w_out[e].astype(BF16))
        else:
            o = layer // 2
            gain = jnp.concatenate([
                jnp.tile(c_q_norm[o], C_HEADS) * scale, jnp.tile(c_k_norm[o], C_HEADS), ones(C_WIDTH)])
            proj = _norm_proj(h, norm_mix[layer], c_w_in[o].astype(BF16), gain, odd_flags)
            proj = proj.reshape(b, seq, -1)
            y = _band_attn(proj, _band_bias_tiles(c_rel_bias[o]))
            h = _out_proj(h, [y.reshape(b * seq, C_WIDTH)], c_w_out[o].astype(BF16))
        h = _ffn(h, norm_ffn[layer], ffn_w_up[layer].astype(BF16), ffn_conv_w[layer], ffn_conv_b[layer],
                 ffn_w_down[layer].astype(BF16), seq)
    return h.reshape(b, seq, d)
```

```python
import functools
import math

import jax
import jax.numpy as jnp
from jax import lax
from jax.experimental import pallas as pl
from jax.experimental.pallas import tpu as pltpu

D_MODEL = 1024
HEAD_DIM = 64
CHUNK = 64
A_HEADS = 4
B_HEADS = 8
C_HEADS = 16
A_WIDTH = A_HEADS * 2 * HEAD_DIM
B_WIDTH = B_HEADS * HEAD_DIM
C_WIDTH = C_HEADS * HEAD_DIM
T5_BUCKETS = 32
T5_MAX_DIST = 128
C_LEFT_CHUNKS = 8
C_MAX_REL = 128
D_FF = 2816
CONV_WIDTH = 3
EPS = 1e-6

LANES = 128
MXU_COLS = 256
ROW_TILE = 512
ATT_BLOCK = 256
CONV_HALO = 16
FAR_GROUP = 4
DIFF_HEADS_PER_STEP = 2
BAND_PAIRS_PER_STEP = 2
STICK_PAIRS_PER_STEP = 2
ONES_ROWS = 16
VMEM_LIMIT = 56 * 1024 * 1024
LOG2_E = math.log2(math.e)

F32 = jnp.float32
BF16 = jnp.bfloat16


def _rms(x, g):
    ms = jnp.mean(x * x, axis=-1, keepdims=True)
    return x * lax.rsqrt(ms + EPS) * g


def _lane_lo(shape):
    return lax.broadcasted_iota(jnp.int32, shape, len(shape) - 1) < HEAD_DIM


def _norm_proj_kernel(x_ref, g_ref, w_ref, gain_ref, o_ref, *, norm_chunks):
    u = _rms(x_ref[...], g_ref[...]).astype(BF16)
    tm = u.shape[0]
    lo = _lane_lo((tm, LANES))
    n_cols = o_ref.shape[1]
    for c in range(n_cols // MXU_COLS):
        acc = jnp.dot(u, w_ref[:, c * MXU_COLS:(c + 1) * MXU_COLS], preferred_element_type=F32)
        for half in range(MXU_COLS // LANES):
            col = c * MXU_COLS + half * LANES
            y = acc[:, half * LANES:(half + 1) * LANES]
            if norm_chunks[col // LANES]:
                sq = y * y
                s0 = jnp.sum(jnp.where(lo, sq, 0.0), axis=-1, keepdims=True)
                s1 = jnp.sum(jnp.where(lo, 0.0, sq), axis=-1, keepdims=True)
                ms = jnp.where(lo, s0, s1) * (1.0 / HEAD_DIM)
                y = y * lax.rsqrt(ms + EPS)
            o_ref[:, col:col + LANES] = (y * gain_ref[:, col:col + LANES]).astype(BF16)


def _norm_proj(h2, g, w, gain, norm_chunks):
    t, d = h2.shape
    n = w.shape[1]
    return pl.pallas_call(
        functools.partial(_norm_proj_kernel, norm_chunks=norm_chunks),
        grid=(t // ROW_TILE,),
        in_specs=[
            pl.BlockSpec((ROW_TILE, d), lambda i: (i, 0)),
            pl.BlockSpec((1, d), lambda i: (0, 0)),
            pl.BlockSpec((d, n), lambda i: (0, 0)),
            pl.BlockSpec((1, n), lambda i: (0, 0)),
        ],
        out_specs=pl.BlockSpec((ROW_TILE, n), lambda i: (i, 0)),
        out_shape=jax.ShapeDtypeStruct((t, n), BF16),
        compiler_params=pltpu.CompilerParams(
            dimension_semantics=("parallel",), vmem_limit_bytes=VMEM_LIMIT),
        name="norm_proj",
    )(h2, g.reshape(1, d), w, gain.reshape(1, n))


def _out_proj_kernel(h_ref, *refs):
    y_refs, w_ref, o_ref = refs[:-2], refs[-2], refs[-1]
    acc = h_ref[...]
    row = 0
    for y_ref in y_refs:
        k = y_ref.shape[1]
        acc = acc + jnp.dot(y_ref[...], w_ref[row:row + k, :], preferred_element_type=F32)
        row += k
    o_ref[...] = acc


def _out_proj(h2, ys, w):
    t, d = h2.shape
    in_specs = [pl.BlockSpec((ROW_TILE, d), lambda i: (i, 0))]
    in_specs += [pl.BlockSpec((ROW_TILE, y.shape[1]), lambda i: (i, 0)) for y in ys]
    in_specs += [pl.BlockSpec(w.shape, lambda i: (0, 0))]
    return pl.pallas_call(
        _out_proj_kernel,
        grid=(t // ROW_TILE,),
        in_specs=in_specs,
        out_specs=pl.BlockSpec((ROW_TILE, d), lambda i: (i, 0)),
        out_shape=jax.ShapeDtypeStruct((t, d), F32),
        compiler_params=pltpu.CompilerParams(
            dimension_semantics=("parallel",), vmem_limit_bytes=VMEM_LIMIT),
        name="out_proj",
    )(h2, *ys, w)


def _ffn_kernel(h_ref, halo_ref, g_ref, wup_ref, cw_ref, cb_ref, wdn_ref, o_ref, hd_ref, *, tiles_per_seq):
    x = h_ref[...]
    g = g_ref[...]
    tm = x.shape[0]
    keep_halo = jnp.where((pl.program_id(0) % tiles_per_seq) != 0, 1.0, 0.0)
    u_ext = jnp.concatenate([_rms(halo_ref[...], g) * keep_halo, _rms(x, g)], axis=0).astype(BF16)

    def conv(col):
        hd_ref[...] = jnp.dot(u_ext, wup_ref[:, col:col + D_FF], preferred_element_type=F32)
        out = cb_ref[:, col:col + D_FF]
        for back in range(CONV_WIDTH):
            tap = cw_ref[CONV_WIDTH - 1 - back:CONV_WIDTH - back, col:col + D_FF]
            out = out + hd_ref[pl.ds(CONV_HALO - back, tm), :] * tap
        return out

    gate = conv(0)
    val = conv(D_FF)
    act = (gate * jax.nn.sigmoid(gate) * val).astype(BF16)
    o_ref[...] = x + jnp.dot(act, wdn_ref[...], preferred_element_type=F32)


def _ffn(h2, g, w_up, conv_w, conv_b, w_down, seq):
    t, d = h2.shape
    halo_blocks = ROW_TILE // CONV_HALO
    const = lambda i: (0, 0)
    return pl.pallas_call(
        functools.partial(_ffn_kernel, tiles_per_seq=seq // ROW_TILE),
        grid=(t // ROW_TILE,),
        in_specs=[
            pl.BlockSpec((ROW_TILE, d), lambda i: (i, 0)),
            pl.BlockSpec((CONV_HALO, d), lambda i: (jnp.maximum(i * halo_blocks - 1, 0), 0)),
            pl.BlockSpec((1, d), const),
            pl.BlockSpec(w_up.shape, const, pipeline_mode=pl.Buffered(1)),
            pl.BlockSpec(conv_w.shape, const),
            pl.BlockSpec((1, 2 * D_FF), const),
            pl.BlockSpec(w_down.shape, const, pipeline_mode=pl.Buffered(1)),
        ],
        out_specs=pl.BlockSpec((ROW_TILE, d), lambda i: (i, 0)),
        out_shape=jax.ShapeDtypeStruct((t, d), F32),
        scratch_shapes=[pltpu.VMEM((ROW_TILE + CONV_HALO, D_FF), F32)],
        compiler_params=pltpu.CompilerParams(
            dimension_semantics=("parallel",), vmem_limit_bytes=VMEM_LIMIT),
        name="conv_glu_ffn",
    )(h2, h2, g.reshape(1, d), w_up, conv_w, conv_b.reshape(1, 2 * D_FF), w_down)


def _split_heads(q):
    lo = _lane_lo(q.shape)
    zero = jnp.zeros_like(q)
    return jnp.concatenate([jnp.where(lo, q, zero), jnp.where(lo, zero, q)], axis=0)


def _nt_dot(a, b):
    return lax.dot_general(a, b, (((1,), (1,)), ((), ())), preferred_element_type=F32)


def _tile_ids(shape):
    return (lax.broadcasted_iota(jnp.int32, shape, 0), lax.broadcasted_iota(jnp.int32, shape, 1))


def _v_tiles(v):
    b, s, w = v.shape
    v = v.reshape(b, s // ATT_BLOCK, ATT_BLOCK, w // LANES, LANES)
    return jnp.transpose(v, (0, 3, 1, 4, 2))


def _v_tiles_with_ones(v):
    vt = _v_tiles(v)
    return jnp.concatenate([vt, jnp.ones(vt.shape[:3] + (ONES_ROWS, ATT_BLOCK), BF16)], axis=3)


def _diff_attn_kernel(lam_ref, q_ref, k_ref, vt_ref, bias_ref, subln_ref, o_ref,
                      s_ref, cmax_ref, m_ref, acc_ref, *, out_scale):
    blk = ATT_BLOCK
    heads = range(DIFF_HEADS_PER_STEP)
    qi = pl.program_id(2)
    lanes = lambda h: slice(h * LANES, (h + 1) * LANES)
    qq = [_split_heads(q_ref[0, :, lanes(h)]) for h in heads]
    m_ref[...] = jnp.full(m_ref.shape, -jnp.inf, F32)
    acc_ref[...] = jnp.zeros(acc_ref.shape, F32)
    n_far = jnp.maximum(qi - 1, 0)
    both = lambda t: jnp.concatenate([t, t], axis=1)

    def scores_to(slot, kj, finish):
        start = pl.multiple_of(kj * blk, blk)
        for h in heads:
            s = finish(h, _nt_dot(k_ref[0, pl.ds(start, blk), lanes(h)], qq[h]))
            s_ref[h, slot] = s
            cmax_ref[h, slot] = jnp.max(s, axis=0, keepdims=True)

    def absorb(slot, kj):
        for h in heads:
            m_prev = m_ref[h]
            m_new = jnp.maximum(m_prev, cmax_ref[h, slot])
            alpha = jnp.exp2(m_prev - m_new)
            p = jnp.exp2(s_ref[h, slot] - m_new).astype(BF16)
            pv = jnp.dot(vt_ref[0, h, kj], p, preferred_element_type=F32)
            acc_ref[h] = alpha * acc_ref[h] + pv
            m_ref[h] = m_new

    def far_key_tile(i):
        return jnp.maximum(qi - 2 - i, 0)

    kk, qc = _tile_ids((blk, blk))
    own_chunks = both((kk >> 6) <= (qc >> 6))
    prev = jnp.maximum(qi - 1, 0)
    no_prev = jnp.where(qi >= 1, 0.0, -jnp.inf)
    scores_to(0, qi, lambda h, s: jnp.where(own_chunks, s + both(bias_ref[h, 0]), -jnp.inf))
    scores_to(1, prev, lambda h, s: s + both(bias_ref[h, 1] + no_prev))
    absorb(0, qi)
    scores_to(0, far_key_tile(0), lambda h, s: s)
    absorb(1, prev)

    def far_trip(first, finish_for):
        for u in range(FAR_GROUP):
            scores_to((u + 1) % 2, far_key_tile(first + u + 1), finish_for(first + u + 1))
            absorb(u % 2, far_key_tile(first + u))

    full_trips = n_far // FAR_GROUP

    def full_trip(g, carry):
        far_trip(g * FAR_GROUP, lambda i: (lambda h, s: s))
        return carry

    lax.fori_loop(0, full_trips, full_trip, 0)

    @pl.when(n_far % FAR_GROUP != 0)
    def _():
        far_trip(full_trips * FAR_GROUP, lambda i: (lambda h, s: s + jnp.where(i < n_far, 0.0, -jnp.inf)))

    for h in heads:
        acc = acc_ref[h]
        o = acc[:LANES] / acc[LANES:LANES + 1]
        ya = o[:, :blk] - lam_ref[0, 0] * o[:, blk:]
        ms = jnp.mean(ya * ya, axis=0, keepdims=True)
        ya = ya * lax.rsqrt(ms + EPS) * (subln_ref[...] * out_scale)
        o_ref[0, :, lanes(h)] = ya.T.astype(BF16)


def _diff_attn(proj, bias_tab, subln, lam, out_scale):
    b, s, _ = proj.shape
    nq = s // ATT_BLOCK
    hps = DIFF_HEADS_PER_STEP
    width = hps * LANES
    q_blk, k_blk = 0, A_WIDTH // width
    vt = _v_tiles_with_ones(proj[:, :, 2 * A_WIDTH:3 * A_WIDTH])
    v_rows = LANES + ONES_ROWS
    return pl.pallas_call(
        functools.partial(_diff_attn_kernel, out_scale=out_scale),
        grid=(b, A_HEADS // hps, nq),
        in_specs=[
            pl.BlockSpec(memory_space=pltpu.SMEM),
            pl.BlockSpec((1, ATT_BLOCK, width), lambda bi, g, qi: (bi, qi, q_blk + g)),
            pl.BlockSpec((1, s, width), lambda bi, g, qi: (bi, 0, k_blk + g)),
            pl.BlockSpec((1, hps, nq, v_rows, ATT_BLOCK), lambda bi, g, qi: (bi, g, 0, 0, 0)),
            pl.BlockSpec((hps, 2, ATT_BLOCK, ATT_BLOCK), lambda bi, g, qi: (g, 0, 0, 0)),
            pl.BlockSpec((LANES, 1), lambda bi, g, qi: (0, 0)),
        ],
        out_specs=pl.BlockSpec((1, ATT_BLOCK, width), lambda bi, g, qi: (bi, qi, g)),
        out_shape=jax.ShapeDtypeStruct((b, s, A_WIDTH), BF16),
        scratch_shapes=[
            pltpu.VMEM((hps, 2, ATT_BLOCK, 2 * ATT_BLOCK), F32),
            pltpu.VMEM((hps, 2, 1, 2 * ATT_BLOCK), F32),
            pltpu.VMEM((hps, 1, 2 * ATT_BLOCK), F32),
            pltpu.VMEM((hps, v_rows, 2 * ATT_BLOCK), F32),
        ],
        compiler_params=pltpu.CompilerParams(
            dimension_semantics=("parallel", "parallel", "arbitrary"), vmem_limit_bytes=VMEM_LIMIT),
        name="diff_attn",
    )(lam.reshape(1, 1), proj, proj, vt, bias_tab, subln.reshape(LANES, 1))


STICK_DEAD_LOGIT = -104.0


def _stick_kernel(q_ref, k_ref, v_ref, o_ref, carry_ref, acc_ref):
    bq = ATT_BLOCK
    pairs = range(STICK_PAIRS_PER_STEP)
    qi = pl.program_id(2)
    lanes = lambda g: slice(g * LANES, (g + 1) * LANES)
    qq = [_split_heads(q_ref[0, :, lanes(g)]) for g in pairs]
    carry_ref[...] = jnp.zeros(carry_ref.shape, F32)
    acc_ref[...] = jnp.zeros(acc_ref.shape, F32)
    r, c = _tile_ids((bq, bq))
    later = jnp.where(r > c, 1.0, 0.0).astype(BF16)
    causal = jnp.concatenate([c < r, c < r], axis=0)

    def update(kj, diagonal):
        start = pl.multiple_of(kj * bq, bq)
        for g in pairs:
            k = k_ref[0, pl.ds(start, bq), lanes(g)]
            v = v_ref[0, pl.ds(start, bq), lanes(g)]
            z = _nt_dot(qq[g], k)
            log_1m = -(jnp.maximum(z, 0.0) + jnp.log(1.0 + jnp.exp(-jnp.abs(z))))
            if diagonal:
                log_1m = jnp.where(causal, log_1m, 0.0)
            hi = log_1m.astype(BF16)
            lo = (log_1m - hi.astype(F32)).astype(BF16)
            remain = (jnp.dot(hi, later, preferred_element_type=F32)
                      + jnp.dot(lo, later, preferred_element_type=F32))
            carry = carry_ref[g]
            a = jnp.exp(z + log_1m + remain + carry)
            if diagonal:
                a = jnp.where(causal, a, 0.0)
            acc_ref[g] += jnp.dot(a.astype(BF16), v, preferred_element_type=F32)
            carry_ref[g] = carry + (remain[:, :1] + log_1m[:, :1])

    def live():
        return jnp.max(carry_ref[...]) > STICK_DEAD_LOGIT

    update(qi, True)

    def body(state):
        j, _ = state
        update(qi - 1 - j, False)
        return j + 1, live()

    lax.while_loop(lambda st: jnp.logical_and(st[0] < qi, st[1]), body, (jnp.int32(0), live()))

    lo_lanes = _lane_lo((bq, LANES))
    for g in pairs:
        acc = acc_ref[g]
        o_ref[0, :, lanes(g)] = jnp.where(lo_lanes, acc[:bq], acc[bq:]).astype(BF16)


def _stick_attn(proj):
    b, s, _ = proj.shape
    nq = s // ATT_BLOCK
    pps = STICK_PAIRS_PER_STEP
    width = pps * LANES
    base = 3 * A_WIDTH // width
    q_blk, k_blk, v_blk = base, base + B_WIDTH // width, base + 2 * B_WIDTH // width
    return pl.pallas_call(
        _stick_kernel,
        grid=(b, B_WIDTH // width, nq),
        in_specs=[
            pl.BlockSpec((1, ATT_BLOCK, width), lambda bi, g, qi: (bi, qi, q_blk + g)),
            pl.BlockSpec((1, s, width), lambda bi, g, qi: (bi, 0, k_blk + g)),
            pl.BlockSpec((1, s, width), lambda bi, g, qi: (bi, 0, v_blk + g)),
        ],
        out_specs=pl.BlockSpec((1, ATT_BLOCK, width), lambda bi, g, qi: (bi, qi, g)),
        out_shape=jax.ShapeDtypeStruct((b, s, B_WIDTH), BF16),
        scratch_shapes=[
            pltpu.VMEM((pps, 2 * ATT_BLOCK, 1), F32),
            pltpu.VMEM((pps, 2 * ATT_BLOCK, LANES), F32),
        ],
        compiler_params=pltpu.CompilerParams(
            dimension_semantics=("parallel", "parallel", "arbitrary"), vmem_limit_bytes=VMEM_LIMIT),
        name="stick_breaking",
    )(proj, proj, proj)


def _band_kernel(q_ref, k_ref, vt_ref, bias_ref, o_ref):
    blk = ATT_BLOCK
    qi = pl.program_id(2)
    kk, qc = _tile_ids((blk, blk))
    kc, qc = kk >> 6, qc >> 6
    both = lambda t: jnp.concatenate([t, t], axis=1)
    masks = (both(kc <= qc), None, both(kc >= qc))
    for g in range(BAND_PAIRS_PER_STEP):
        lanes = slice(g * LANES, (g + 1) * LANES)
        qq = _split_heads(q_ref[0, :, lanes])
        scores, vts = [], []
        for d in range(3):
            kj = jnp.maximum(qi - d, 0)
            start = pl.multiple_of(kj * blk, blk)
            vts.append(vt_ref[0, g, kj])
            s = _nt_dot(k_ref[0, pl.ds(start, blk), lanes], qq)
            s = s + jnp.concatenate([bias_ref[2 * g, d], bias_ref[2 * g + 1, d]], axis=1)
            if d > 0:
                s = s + jnp.where(qi >= d, 0.0, -jnp.inf)
            if masks[d] is not None:
                s = jnp.where(masks[d], s, -jnp.inf)
            scores.append(s)
        m = functools.reduce(jnp.maximum, [jnp.max(s, axis=0, keepdims=True) for s in scores])
        acc = jnp.zeros((LANES + ONES_ROWS, 2 * blk), F32)
        for s, vt in zip(scores, vts):
            acc = acc + jnp.dot(vt, jnp.exp2(s - m).astype(BF16), preferred_element_type=F32)
        o = acc[:LANES] / acc[LANES:LANES + 1]
        o = jnp.concatenate([o[:HEAD_DIM, :blk], o[HEAD_DIM:, blk:]], axis=0)
        o_ref[0, :, lanes] = o.T.astype(BF16)


def _band_attn(proj, bias_tab):
    b, s, _ = proj.shape
    nq = s // ATT_BLOCK
    pps = BAND_PAIRS_PER_STEP
    width = pps * LANES
    q_blk, k_blk = 0, C_WIDTH // width
    vt = _v_tiles_with_ones(proj[:, :, 2 * C_WIDTH:3 * C_WIDTH])
    return pl.pallas_call(
        _band_kernel,
        grid=(b, C_WIDTH // width, nq),
        in_specs=[
            pl.BlockSpec((1, ATT_BLOCK, width), lambda bi, g, qi: (bi, qi, q_blk + g)),
            pl.BlockSpec((1, s, width), lambda bi, g, qi: (bi, 0, k_blk + g)),
            pl.BlockSpec((1, pps, nq, LANES + ONES_ROWS, ATT_BLOCK), lambda bi, g, qi: (bi, g, 0, 0, 0)),
            pl.BlockSpec((2 * pps, 3, ATT_BLOCK, ATT_BLOCK), lambda bi, g, qi: (g, 0, 0, 0)),
        ],
        out_specs=pl.BlockSpec((1, ATT_BLOCK, width), lambda bi, g, qi: (bi, qi, g)),
        out_shape=jax.ShapeDtypeStruct((b, s, C_WIDTH), BF16),
        compiler_params=pltpu.CompilerParams(
            dimension_semantics=("parallel", "parallel", "arbitrary"), vmem_limit_bytes=VMEM_LIMIT),
        name="chunk_band",
    )(proj, proj, vt, bias_tab)


def _t5_bucket(rel):
    nb = T5_BUCKETS // 2
    max_exact = nb // 2
    n = jnp.abs(rel)
    large = max_exact + (jnp.log(jnp.maximum(n, 1).astype(jnp.float32) / max_exact)
                         / math.log(T5_MAX_DIST / max_exact) * (nb - max_exact)).astype(jnp.int32)
    large = jnp.minimum(large, nb - 1)
    return jnp.where(rel > 0, nb, 0) + jnp.where(n < max_exact, n, large)


def _toeplitz_tiles(per_distance):
    blk = ATT_BLOCK
    vec = per_distance.T.astype(F32)
    h, period = vec.shape
    skew = jnp.tile(vec, (1, blk))[:, :blk * (period - 1)].reshape(h, blk, period - 1)
    return jnp.stack([skew[:, :, blk - 1 + blk * d:2 * blk - 1 + blk * d] for d in range(3)], axis=1)


def _tile_distances():
    return jnp.arange(4 * ATT_BLOCK) - (ATT_BLOCK - 1)


def _t5_bias_tiles(t5_bias):
    assert ATT_BLOCK >= T5_MAX_DIST
    tiles = _toeplitz_tiles(t5_bias[_t5_bucket(-_tile_distances())])
    far = tiles[:, 2, :1, :1]
    return (tiles[:, :2] - far[:, None]) * LOG2_E


def _band_bias_tiles(rel_bias):
    return _toeplitz_tiles(rel_bias[jnp.clip(_tile_distances(), -C_MAX_REL, C_MAX_REL) + C_MAX_REL]) * LOG2_E


def _chunk_flags(*sections):
    flags = []
    for width, normed in sections:
        flags += [normed] * (width // LANES)
    return tuple(flags)


def kernel(x, t5_bias, norm_mix, norm_ffn, ab_w_in, a_q_norm, a_k_norm, a_lambda, a_subln, ab_w_out,
           c_w_in, c_q_norm, c_k_norm, c_rel_bias, c_w_out, ffn_w_up, ffn_conv_w, ffn_conv_b, ffn_w_down):
    b, seq, d = x.shape
    depth = norm_mix.shape[0]
    assert d == D_MODEL and seq % ROW_TILE == 0 and seq % ATT_BLOCK == 0
    scale = HEAD_DIM ** -0.5
    h = x.reshape(b * seq, d)
    t5_tiles = _t5_bias_tiles(t5_bias)
    ones = lambda n: jnp.ones((n,), F32)
    even_flags = _chunk_flags((2 * A_WIDTH, True), (A_WIDTH + 3 * B_WIDTH, False))
    odd_flags = _chunk_flags((2 * C_WIDTH, True), (C_WIDTH, False))
    for layer in range(depth):
        if layer % 2 == 0:
            e = layer // 2
            lam_init = 0.8 - 0.6 * math.exp(-0.3 * layer)
            gain = jnp.concatenate([
                jnp.tile(a_q_norm[e], 2 * A_HEADS) * (scale * LOG2_E), jnp.tile(a_k_norm[e], 2 * A_HEADS), ones(A_WIDTH),
                ones(B_WIDTH) * scale, ones(2 * B_WIDTH)])
            proj = _norm_proj(h, norm_mix[layer], ab_w_in[e].astype(BF16), gain, even_flags)
            proj = proj.reshape(b, seq, -1)
            lam_p = a_lambda[e]
            lam = jnp.exp(jnp.sum(lam_p[0] * lam_p[1])) - jnp.exp(jnp.sum(lam_p[2] * lam_p[3])) + lam_init
            ya = _diff_attn(proj, t5_tiles, a_subln[e], lam.astype(F32), 1.0 - lam_init)
            yb = _stick_attn(proj)
            ys = [ya.reshape(b * seq, A_WIDTH), yb.reshape(b * seq, B_WIDTH)]
            h = _out_proj(h, ys, ab_w_out[e].astype(BF16))
        else:
            o = layer // 2
            gain = jnp.concatenate([
                jnp.tile(c_q_norm[o], C_HEADS) * (scale * LOG2_E), jnp.tile(c_k_norm[o], C_HEADS), ones(C_WIDTH)])
            proj = _norm_proj(h, norm_mix[layer], c_w_in[o].astype(BF16), gain, odd_flags)
            proj = proj.reshape(b, seq, -1)
            y = _band_attn(proj, _band_bias_tiles(c_rel_bias[o]))
            h = _out_proj(h, [y.reshape(b * seq, C_WIDTH)], c_w_out[o].astype(BF16))
        h = _ffn(h, norm_ffn[layer], ffn_w_up[layer].astype(BF16), ffn_conv_w[layer], ffn_conv_b[layer],
                 ffn_w_down[layer].astype(BF16), seq)
    return h.reshape(b, seq, d)
```

```python
import functools
import math

import jax
import jax.numpy as jnp
from jax import lax
from jax.experimental import pallas as pl
from jax.experimental.pallas import tpu as pltpu

D_MODEL = 1024
HEAD_DIM = 64
CHUNK = 64
A_HEADS = 4
B_HEADS = 8
C_HEADS = 16
A_WIDTH = A_HEADS * 2 * HEAD_DIM
B_WIDTH = B_HEADS * HEAD_DIM
C_WIDTH = C_HEADS * HEAD_DIM
T5_BUCKETS = 32
T5_MAX_DIST = 128
C_LEFT_CHUNKS = 8
C_MAX_REL = 128
D_FF = 2816
CONV_WIDTH = 3
EPS = 1e-6

LANES = 128
MXU_COLS = 256
ROW_TILE = 512
ATT_BLOCK = 256
CONV_HALO = 16
FAR_GROUP = 4
DIFF_HEADS_PER_STEP = 2
BAND_PAIRS_PER_STEP = 4
STICK_PAIRS_PER_STEP = 2
ONES_ROWS = 16
VMEM_LIMIT = 56 * 1024 * 1024
LOG2_E = math.log2(math.e)

F32 = jnp.float32
BF16 = jnp.bfloat16


def _rms(x, g):
    ms = jnp.mean(x * x, axis=-1, keepdims=True)
    return x * lax.rsqrt(ms + EPS) * g


def _nt_dot(a, b):
    return lax.dot_general(a, b, (((1,), (1,)), ((), ())), preferred_element_type=F32)


def _lane_lo(shape):
    return lax.broadcasted_iota(jnp.int32, shape, len(shape) - 1) < HEAD_DIM


def _norm_proj_kernel(x_ref, g_ref, w_ref, gain_ref, wvt_ref, o_ref, vt_ref, *, norm_chunks):
    u = _rms(x_ref[...], g_ref[...]).astype(BF16)
    tm = u.shape[0]
    lo = _lane_lo((tm, LANES))
    n_cols = o_ref.shape[1]
    for c in range(n_cols // MXU_COLS):
        acc = jnp.dot(u, w_ref[:, c * MXU_COLS:(c + 1) * MXU_COLS], preferred_element_type=F32)
        for half in range(MXU_COLS // LANES):
            col = c * MXU_COLS + half * LANES
            y = acc[:, half * LANES:(half + 1) * LANES]
            if norm_chunks[col // LANES]:
                sq = y * y
                s0 = jnp.sum(jnp.where(lo, sq, 0.0), axis=-1, keepdims=True)
                s1 = jnp.sum(jnp.where(lo, 0.0, sq), axis=-1, keepdims=True)
                ms = jnp.where(lo, s0, s1) * (1.0 / HEAD_DIM)
                y = y * lax.rsqrt(ms + EPS)
            o_ref[:, col:col + LANES] = (y * gain_ref[:, col:col + LANES]).astype(BF16)
    vt = _nt_dot(wvt_ref[...], u).astype(BF16)
    for grp in range(vt_ref.shape[1]):
        for t in range(tm // ATT_BLOCK):
            vt_ref[0, grp, t, :LANES, :] = vt[grp * LANES:(grp + 1) * LANES, t * ATT_BLOCK:(t + 1) * ATT_BLOCK]
            vt_ref[0, grp, t, LANES:, :] = jnp.ones((ONES_ROWS, ATT_BLOCK), BF16)


def _norm_proj(h2, g, w, gain, norm_chunks, w_vt, seq):
    t, d = h2.shape
    n = w.shape[1]
    groups = w_vt.shape[0] // LANES
    tiles_per_seq = seq // ROW_TILE
    tiles_per_step = ROW_TILE // ATT_BLOCK
    v_rows = LANES + ONES_ROWS
    return pl.pallas_call(
        functools.partial(_norm_proj_kernel, norm_chunks=norm_chunks),
        grid=(t // ROW_TILE,),
        in_specs=[
            pl.BlockSpec((ROW_TILE, d), lambda i: (i, 0)),
            pl.BlockSpec((1, d), lambda i: (0, 0)),
            pl.BlockSpec((d, n), lambda i: (0, 0)),
            pl.BlockSpec((1, n), lambda i: (0, 0)),
            pl.BlockSpec(w_vt.shape, lambda i: (0, 0)),
        ],
        out_specs=[
            pl.BlockSpec((ROW_TILE, n), lambda i: (i, 0)),
            pl.BlockSpec((1, groups, tiles_per_step, v_rows, ATT_BLOCK),
                         lambda i: (i // tiles_per_seq, 0, i % tiles_per_seq, 0, 0)),
        ],
        out_shape=[
            jax.ShapeDtypeStruct((t, n), BF16),
            jax.ShapeDtypeStruct((t // seq, groups, seq // ATT_BLOCK, v_rows, ATT_BLOCK), BF16),
        ],
        compiler_params=pltpu.CompilerParams(
            dimension_semantics=("parallel",), vmem_limit_bytes=VMEM_LIMIT),
        name="norm_proj",
    )(h2, g.reshape(1, d), w, gain.reshape(1, n), w_vt)


def _out_proj_kernel(h_ref, *refs):
    y_refs, w_ref, o_ref = refs[:-2], refs[-2], refs[-1]
    acc = h_ref[...]
    row = 0
    for y_ref in y_refs:
        k = y_ref.shape[1]
        acc = acc + jnp.dot(y_ref[...], w_ref[row:row + k, :], preferred_element_type=F32)
        row += k
    o_ref[...] = acc


def _out_proj(h2, ys, w):
    t, d = h2.shape
    in_specs = [pl.BlockSpec((ROW_TILE, d), lambda i: (i, 0))]
    in_specs += [pl.BlockSpec((ROW_TILE, y.shape[1]), lambda i: (i, 0)) for y in ys]
    in_specs += [pl.BlockSpec(w.shape, lambda i: (0, 0))]
    return pl.pallas_call(
        _out_proj_kernel,
        grid=(t // ROW_TILE,),
        in_specs=in_specs,
        out_specs=pl.BlockSpec((ROW_TILE, d), lambda i: (i, 0)),
        out_shape=jax.ShapeDtypeStruct((t, d), F32),
        compiler_params=pltpu.CompilerParams(
            dimension_semantics=("parallel",), vmem_limit_bytes=VMEM_LIMIT),
        name="out_proj",
    )(h2, *ys, w)


def _ffn_kernel(h_ref, halo_ref, g_ref, wup_ref, cw_ref, cb_ref, wdn_ref, o_ref, hd_ref, *, tiles_per_seq):
    x = h_ref[...]
    g = g_ref[...]
    tm = x.shape[0]
    keep_halo = jnp.where((pl.program_id(0) % tiles_per_seq) != 0, 1.0, 0.0)
    u_ext = jnp.concatenate([_rms(halo_ref[...], g) * keep_halo, _rms(x, g)], axis=0).astype(BF16)

    def conv(col):
        hd_ref[...] = jnp.dot(u_ext, wup_ref[:, col:col + D_FF], preferred_element_type=F32)
        out = cb_ref[:, col:col + D_FF]
        for back in range(CONV_WIDTH):
            tap = cw_ref[CONV_WIDTH - 1 - back:CONV_WIDTH - back, col:col + D_FF]
            out = out + hd_ref[pl.ds(CONV_HALO - back, tm), :] * tap
        return out

    gate = conv(0)
    val = conv(D_FF)
    act = (gate * jax.nn.sigmoid(gate) * val).astype(BF16)
    o_ref[...] = x + jnp.dot(act, wdn_ref[...], preferred_element_type=F32)


def _ffn(h2, g, w_up, conv_w, conv_b, w_down, seq):
    t, d = h2.shape
    halo_blocks = ROW_TILE // CONV_HALO
    const = lambda i: (0, 0)
    return pl.pallas_call(
        functools.partial(_ffn_kernel, tiles_per_seq=seq // ROW_TILE),
        grid=(t // ROW_TILE,),
        in_specs=[
            pl.BlockSpec((ROW_TILE, d), lambda i: (i, 0)),
            pl.BlockSpec((CONV_HALO, d), lambda i: (jnp.maximum(i * halo_blocks - 1, 0), 0)),
            pl.BlockSpec((1, d), const),
            pl.BlockSpec(w_up.shape, const, pipeline_mode=pl.Buffered(1)),
            pl.BlockSpec(conv_w.shape, const),
            pl.BlockSpec((1, 2 * D_FF), const),
            pl.BlockSpec(w_down.shape, const, pipeline_mode=pl.Buffered(1)),
        ],
        out_specs=pl.BlockSpec((ROW_TILE, d), lambda i: (i, 0)),
        out_shape=jax.ShapeDtypeStruct((t, d), F32),
        scratch_shapes=[pltpu.VMEM((ROW_TILE + CONV_HALO, D_FF), F32)],
        compiler_params=pltpu.CompilerParams(
            dimension_semantics=("parallel",), vmem_limit_bytes=VMEM_LIMIT),
        name="conv_glu_ffn",
    )(h2, h2, g.reshape(1, d), w_up, conv_w, conv_b.reshape(1, 2 * D_FF), w_down)


def _split_heads(q):
    lo = _lane_lo(q.shape)
    zero = jnp.zeros_like(q)
    return jnp.concatenate([jnp.where(lo, q, zero), jnp.where(lo, zero, q)], axis=0)


def _tile_ids(shape):
    return (lax.broadcasted_iota(jnp.int32, shape, 0), lax.broadcasted_iota(jnp.int32, shape, 1))


def _diff_attn_kernel(lam_ref, q_ref, k_ref, vt_ref, bias_ref, subln_ref, o_ref,
                      s_ref, cmax_ref, m_ref, acc_ref, *, out_scale):
    blk = ATT_BLOCK
    heads = range(DIFF_HEADS_PER_STEP)
    qi = pl.program_id(2)
    lanes = lambda h: slice(h * LANES, (h + 1) * LANES)
    qq = [_split_heads(q_ref[0, :, lanes(h)]) for h in heads]
    m_ref[...] = jnp.full(m_ref.shape, -jnp.inf, F32)
    acc_ref[...] = jnp.zeros(acc_ref.shape, F32)
    n_far = jnp.maximum(qi - 1, 0)
    both = lambda t: jnp.concatenate([t, t], axis=1)

    def scores_to(slot, kj, finish):
        start = pl.multiple_of(kj * blk, blk)
        for h in heads:
            s = finish(h, _nt_dot(k_ref[0, pl.ds(start, blk), lanes(h)], qq[h]))
            s_ref[h, slot] = s
            cmax_ref[h, slot] = jnp.max(s_ref[h, slot], axis=0, keepdims=True)

    def absorb(slot, kj):
        for h in heads:
            m_prev = m_ref[h]
            m_new = jnp.maximum(m_prev, cmax_ref[h, slot])
            alpha = jnp.exp2(m_prev - m_new)
            p = jnp.exp2(s_ref[h, slot] - m_new).astype(BF16)
            pv = jnp.dot(vt_ref[0, h, kj], p, preferred_element_type=F32)
            acc_ref[h] = alpha * acc_ref[h] + pv
            m_ref[h] = m_new

    def far_key_tile(i):
        return jnp.maximum(qi - 2 - i, 0)

    prev = jnp.maximum(qi - 1, 0)
    no_prev = jnp.where(qi >= 1, 0.0, -jnp.inf)
    scores_to(0, qi, lambda h, s: s + both(bias_ref[h, 0]))
    scores_to(1, prev, lambda h, s: s + both(bias_ref[h, 1] + no_prev))
    absorb(0, qi)
    scores_to(0, far_key_tile(0), lambda h, s: s)
    absorb(1, prev)

    def far_trip(first, finish_for):
        for u in range(FAR_GROUP):
            scores_to((u + 1) % 2, far_key_tile(first + u + 1), finish_for(first + u + 1))
            absorb(u % 2, far_key_tile(first + u))

    full_trips = n_far // FAR_GROUP

    def full_trip(g, carry):
        far_trip(g * FAR_GROUP, lambda i: (lambda h, s: s))
        return carry

    lax.fori_loop(0, full_trips, full_trip, 0)

    @pl.when(n_far % FAR_GROUP != 0)
    def _():
        far_trip(full_trips * FAR_GROUP, lambda i: (lambda h, s: s + jnp.where(i < n_far, 0.0, -jnp.inf)))

    for h in heads:
        acc = acc_ref[h]
        o = acc[:LANES] / acc[LANES:LANES + 1]
        ya = o[:, :blk] - lam_ref[0, 0] * o[:, blk:]
        ms = jnp.mean(ya * ya, axis=0, keepdims=True)
        ya = ya * lax.rsqrt(ms + EPS) * (subln_ref[...] * out_scale)
        o_ref[0, :, lanes(h)] = ya.T.astype(BF16)


def _diff_attn(proj, vt, bias_tab, subln, lam, out_scale):
    b, s, _ = proj.shape
    nq = s // ATT_BLOCK
    hps = DIFF_HEADS_PER_STEP
    width = hps * LANES
    q_blk, k_blk = 0, A_WIDTH // width
    v_rows = LANES + ONES_ROWS
    return pl.pallas_call(
        functools.partial(_diff_attn_kernel, out_scale=out_scale),
        grid=(b, A_HEADS // hps, nq),
        in_specs=[
            pl.BlockSpec(memory_space=pltpu.SMEM),
            pl.BlockSpec((1, ATT_BLOCK, width), lambda bi, g, qi: (bi, qi, q_blk + g)),
            pl.BlockSpec((1, s, width), lambda bi, g, qi: (bi, 0, k_blk + g)),
            pl.BlockSpec((1, hps, nq, v_rows, ATT_BLOCK), lambda bi, g, qi: (bi, g, 0, 0, 0)),
            pl.BlockSpec((hps, 2, ATT_BLOCK, ATT_BLOCK), lambda bi, g, qi: (g, 0, 0, 0)),
            pl.BlockSpec((LANES, 1), lambda bi, g, qi: (0, 0)),
        ],
        out_specs=pl.BlockSpec((1, ATT_BLOCK, width), lambda bi, g, qi: (bi, qi, g)),
        out_shape=jax.ShapeDtypeStruct((b, s, A_WIDTH), BF16),
        scratch_shapes=[
            pltpu.VMEM((hps, 2, ATT_BLOCK, 2 * ATT_BLOCK), F32),
            pltpu.VMEM((hps, 2, 1, 2 * ATT_BLOCK), F32),
            pltpu.VMEM((hps, 1, 2 * ATT_BLOCK), F32),
            pltpu.VMEM((hps, v_rows, 2 * ATT_BLOCK), F32),
        ],
        compiler_params=pltpu.CompilerParams(
            dimension_semantics=("parallel", "parallel", "arbitrary"), vmem_limit_bytes=VMEM_LIMIT),
        name="diff_attn",
    )(lam.reshape(1, 1), proj, proj, vt, bias_tab, subln.reshape(LANES, 1))


STICK_DEAD_LOGIT = -104.0


def _stick_kernel(q_ref, k_ref, vt_ref, o_ref, carry_ref, acc_ref):
    blk = ATT_BLOCK
    pairs = range(STICK_PAIRS_PER_STEP)
    qi = pl.program_id(2)
    lanes = lambda g: slice(g * LANES, (g + 1) * LANES)
    qq = [_split_heads(q_ref[0, :, lanes(g)]) for g in pairs]
    carry_ref[...] = jnp.zeros(carry_ref.shape, F32)
    acc_ref[...] = jnp.zeros(acc_ref.shape, F32)
    kk, qc = _tile_ids((blk, blk))
    later = jnp.where(qc > kk, 1.0, 0.0).astype(BF16)
    causal = jnp.concatenate([kk < qc, kk < qc], axis=1)

    def update(kj, diagonal):
        start = pl.multiple_of(kj * blk, blk)
        logits = []
        for g in pairs:
            z = _nt_dot(k_ref[0, pl.ds(start, blk), lanes(g)], qq[g])
            log_1m = -(jnp.maximum(z, 0.0) + jnp.log(1.0 + jnp.exp(-jnp.abs(z))))
            if diagonal:
                log_1m = jnp.where(causal, log_1m, 0.0)
            hi = log_1m.astype(BF16)
            lo = (log_1m - hi.astype(F32)).astype(BF16)
            remain = (jnp.dot(later, hi, preferred_element_type=F32)
                      + jnp.dot(later, lo, preferred_element_type=F32))
            carry = carry_ref[g]
            logits.append(z + log_1m + remain + carry)
            carry_ref[g] = carry + (remain[:1] + log_1m[:1])
        for g in pairs:
            a = jnp.exp(logits[g])
            if diagonal:
                a = jnp.where(causal, a, 0.0)
            acc_ref[g] += jnp.dot(vt_ref[0, g, kj, :LANES, :], a.astype(BF16), preferred_element_type=F32)

    def live():
        return jnp.max(carry_ref[...]) > STICK_DEAD_LOGIT

    update(qi, True)

    def body(state):
        j, _ = state
        update(qi - 1 - j, False)
        return j + 1, live()

    lax.while_loop(lambda st: jnp.logical_and(st[0] < qi, st[1]), body, (jnp.int32(0), live()))

    for g in pairs:
        acc = acc_ref[g]
        o = jnp.concatenate([acc[:HEAD_DIM, :blk], acc[HEAD_DIM:, blk:]], axis=0)
        o_ref[0, :, lanes(g)] = o.T.astype(BF16)


def _stick_attn(proj, vt):
    b, s, _ = proj.shape
    nq = s // ATT_BLOCK
    pps = STICK_PAIRS_PER_STEP
    width = pps * LANES
    base = 2 * A_WIDTH // width
    q_blk, k_blk = base, base + B_WIDTH // width
    vt_blk = A_HEADS // pps
    return pl.pallas_call(
        _stick_kernel,
        grid=(b, B_WIDTH // width, nq),
        in_specs=[
            pl.BlockSpec((1, ATT_BLOCK, width), lambda bi, g, qi: (bi, qi, q_blk + g)),
            pl.BlockSpec((1, s, width), lambda bi, g, qi: (bi, 0, k_blk + g)),
            pl.BlockSpec((1, pps, nq, LANES + ONES_ROWS, ATT_BLOCK), lambda bi, g, qi: (bi, vt_blk + g, 0, 0, 0)),
        ],
        out_specs=pl.BlockSpec((1, ATT_BLOCK, width), lambda bi, g, qi: (bi, qi, g)),
        out_shape=jax.ShapeDtypeStruct((b, s, B_WIDTH), BF16),
        scratch_shapes=[
            pltpu.VMEM((pps, 1, 2 * ATT_BLOCK), F32),
            pltpu.VMEM((pps, LANES, 2 * ATT_BLOCK), F32),
        ],
        compiler_params=pltpu.CompilerParams(
            dimension_semantics=("parallel", "parallel", "arbitrary"), vmem_limit_bytes=VMEM_LIMIT),
        name="stick_breaking",
    )(proj, proj, vt)


def _band_kernel(q_ref, k_ref, vt_ref, bias_ref, o_ref, s_ref):
    blk = ATT_BLOCK
    qi = pl.program_id(2)
    key_tiles = [jnp.maximum(qi - d, 0) for d in range(3)]
    padding = [None] + [jnp.where(qi >= d, 0.0, -jnp.inf) for d in (1, 2)]

    def score(g):
        lanes = slice(g * LANES, (g + 1) * LANES)
        qq = _split_heads(q_ref[0, :, lanes])
        tile_max = []
        for d in range(3):
            start = pl.multiple_of(key_tiles[d] * blk, blk)
            s = _nt_dot(k_ref[0, pl.ds(start, blk), lanes], qq)
            bias = jnp.concatenate([bias_ref[2 * g, d], bias_ref[2 * g + 1, d]], axis=1)
            s = s + (bias if padding[d] is None else bias + padding[d])
            s_ref[g, d] = s
            tile_max.append(jnp.max(s_ref[g, d], axis=0, keepdims=True))
        return functools.reduce(jnp.maximum, tile_max)

    def attend(g, m):
        lanes = slice(g * LANES, (g + 1) * LANES)
        acc = jnp.zeros((LANES + ONES_ROWS, 2 * blk), F32)
        for d in range(3):
            p = jnp.exp2(s_ref[g, d] - m).astype(BF16)
            acc = acc + jnp.dot(vt_ref[0, g, key_tiles[d]], p, preferred_element_type=F32)
        o = acc[:LANES] / acc[LANES:LANES + 1]
        o = jnp.concatenate([o[:HEAD_DIM, :blk], o[HEAD_DIM:, blk:]], axis=0)
        o_ref[0, :, lanes] = o.T.astype(BF16)

    pending = score(0)
    for g in range(BAND_PAIRS_PER_STEP):
        following = score(g + 1) if g + 1 < BAND_PAIRS_PER_STEP else None
        attend(g, pending)
        pending = following


def _band_attn(proj, vt, bias_tab):
    b, s, _ = proj.shape
    nq = s // ATT_BLOCK
    pps = BAND_PAIRS_PER_STEP
    width = pps * LANES
    q_blk, k_blk = 0, C_WIDTH // width
    return pl.pallas_call(
        _band_kernel,
        grid=(b, C_WIDTH // width, nq),
        in_specs=[
            pl.BlockSpec((1, ATT_BLOCK, width), lambda bi, g, qi: (bi, qi, q_blk + g)),
            pl.BlockSpec((1, s, width), lambda bi, g, qi: (bi, 0, k_blk + g)),
            pl.BlockSpec((1, pps, nq, LANES + ONES_ROWS, ATT_BLOCK), lambda bi, g, qi: (bi, g, 0, 0, 0)),
            pl.BlockSpec((2 * pps, 3, ATT_BLOCK, ATT_BLOCK), lambda bi, g, qi: (g, 0, 0, 0)),
        ],
        out_specs=pl.BlockSpec((1, ATT_BLOCK, width), lambda bi, g, qi: (bi, qi, g)),
        out_shape=jax.ShapeDtypeStruct((b, s, C_WIDTH), BF16),
        scratch_shapes=[pltpu.VMEM((pps, 3, ATT_BLOCK, 2 * ATT_BLOCK), F32)],
        compiler_params=pltpu.CompilerParams(
            dimension_semantics=("parallel", "parallel", "arbitrary"), vmem_limit_bytes=VMEM_LIMIT),
        name="chunk_band",
    )(proj, proj, vt, bias_tab)


def _t5_bucket(rel):
    nb = T5_BUCKETS // 2
    max_exact = nb // 2
    n = jnp.abs(rel)
    large = max_exact + (jnp.log(jnp.maximum(n, 1).astype(jnp.float32) / max_exact)
                         / math.log(T5_MAX_DIST / max_exact) * (nb - max_exact)).astype(jnp.int32)
    large = jnp.minimum(large, nb - 1)
    return jnp.where(rel > 0, nb, 0) + jnp.where(n < max_exact, n, large)


def _toeplitz_tiles(per_distance):
    blk = ATT_BLOCK
    vec = per_distance.T.astype(F32)
    h, period = vec.shape
    skew = jnp.tile(vec, (1, blk))[:, :blk * (period - 1)].reshape(h, blk, period - 1)
    return jnp.stack([skew[:, :, blk - 1 + blk * d:2 * blk - 1 + blk * d] for d in range(3)], axis=1)


def _tile_distances():
    return jnp.arange(4 * ATT_BLOCK) - (ATT_BLOCK - 1)


def _tile_chunks():
    i = jnp.arange(ATT_BLOCK) // CHUNK
    return i[:, None], i[None, :]


def _t5_bias_tiles(t5_bias):
    assert ATT_BLOCK >= T5_MAX_DIST
    tiles = _toeplitz_tiles(t5_bias[_t5_bucket(-_tile_distances())])
    far = tiles[:, 2, :1, :1]
    tiles = (tiles[:, :2] - far[:, None]) * LOG2_E
    key_chunk, query_chunk = _tile_chunks()
    return tiles.at[:, 0].add(jnp.where(key_chunk <= query_chunk, 0.0, -jnp.inf))


def _band_bias_tiles(rel_bias):
    assert C_LEFT_CHUNKS * CHUNK == 2 * ATT_BLOCK
    tiles = _toeplitz_tiles(rel_bias[jnp.clip(_tile_distances(), -C_MAX_REL, C_MAX_REL) + C_MAX_REL]) * LOG2_E
    key_chunk, query_chunk = _tile_chunks()
    tiles = tiles.at[:, 0].add(jnp.where(key_chunk <= query_chunk, 0.0, -jnp.inf))
    return tiles.at[:, 2].add(jnp.where(key_chunk >= query_chunk, 0.0, -jnp.inf))


def _chunk_flags(*sections):
    flags = []
    for width, normed in sections:
        flags += [normed] * (width // LANES)
    return tuple(flags)


def kernel(x, t5_bias, norm_mix, norm_ffn, ab_w_in, a_q_norm, a_k_norm, a_lambda, a_subln, ab_w_out,
           c_w_in, c_q_norm, c_k_norm, c_rel_bias, c_w_out, ffn_w_up, ffn_conv_w, ffn_conv_b, ffn_w_down):
    b, seq, d = x.shape
    depth = norm_mix.shape[0]
    assert d == D_MODEL and seq % ROW_TILE == 0 and seq % ATT_BLOCK == 0
    scale = HEAD_DIM ** -0.5
    h = x.reshape(b * seq, d)
    t5_tiles = _t5_bias_tiles(t5_bias)
    ones = lambda n: jnp.ones((n,), F32)
    even_flags = _chunk_flags((2 * A_WIDTH, True), (2 * B_WIDTH, False))
    odd_flags = _chunk_flags((2 * C_WIDTH, True))
    for layer in range(depth):
        if layer % 2 == 0:
            e = layer // 2
            lam_init = 0.8 - 0.6 * math.exp(-0.3 * layer)
            w_in = ab_w_in[e].astype(BF16)
            gain = jnp.concatenate([
                jnp.tile(a_q_norm[e], 2 * A_HEADS) * (scale * LOG2_E), jnp.tile(a_k_norm[e], 2 * A_HEADS),
                ones(B_WIDTH) * scale, ones(B_WIDTH)])
            qk_cols = 2 * A_WIDTH
            w_main = jnp.concatenate([w_in[:, :qk_cols], w_in[:, 3 * A_WIDTH:3 * A_WIDTH + 2 * B_WIDTH]], axis=1)
            w_v = jnp.concatenate([w_in[:, qk_cols:3 * A_WIDTH], w_in[:, 3 * A_WIDTH + 2 * B_WIDTH:]], axis=1)
            proj, vt = _norm_proj(h, norm_mix[layer], w_main, gain, even_flags, w_v.T, seq)
            proj = proj.reshape(b, seq, -1)
            lam_p = a_lambda[e]
            lam = jnp.exp(jnp.sum(lam_p[0] * lam_p[1])) - jnp.exp(jnp.sum(lam_p[2] * lam_p[3])) + lam_init
            ya = _diff_attn(proj, vt, t5_tiles, a_subln[e], lam.astype(F32), 1.0 - lam_init)
            yb = _stick_attn(proj, vt)
            ys = [ya.reshape(b * seq, A_WIDTH), yb.reshape(b * seq, B_WIDTH)]
            h = _out_proj(h, ys, ab_w_out[e].astype(BF16))
        else:
            o = layer // 2
            w_in = c_w_in[o].astype(BF16)
            gain = jnp.concatenate([
                jnp.tile(c_q_norm[o], C_HEADS) * (scale * LOG2_E), jnp.tile(c_k_norm[o], C_HEADS)])
            proj, vt = _norm_proj(h, norm_mix[layer], w_in[:, :2 * C_WIDTH], gain, odd_flags,
                                  w_in[:, 2 * C_WIDTH:].T, seq)
            proj = proj.reshape(b, seq, -1)
            y = _band_attn(proj, vt, _band_bias_tiles(c_rel_bias[o]))
            h = _out_proj(h, [y.reshape(b * seq, C_WIDTH)], c_w_out[o].astype(BF16))
        h = _ffn(h, norm_ffn[layer], ffn_w_up[layer].astype(BF16), ffn_conv_w[layer], ffn_conv_b[layer],
                 ffn_w_down[layer].astype(BF16), seq)
    return h.reshape(b, seq, d)
```

```python
import functools
import math

import jax
import jax.numpy as jnp
from jax import lax
from jax.experimental import pallas as pl
from jax.experimental.pallas import tpu as pltpu

D_MODEL = 1024
HEAD_DIM = 64
CHUNK = 64
A_HEADS = 4
B_HEADS = 8
C_HEADS = 16
A_WIDTH = A_HEADS * 2 * HEAD_DIM
B_WIDTH = B_HEADS * HEAD_DIM
C_WIDTH = C_HEADS * HEAD_DIM
T5_BUCKETS = 32
T5_MAX_DIST = 128
C_LEFT_CHUNKS = 8
C_MAX_REL = 128
D_FF = 2816
CONV_WIDTH = 3
EPS = 1e-6

LANES = 128
MXU_COLS = 256
ROW_TILE = 512
ATT_BLOCK = 256
CONV_HALO = 16
FAR_GROUP = 4
DIFF_HEADS_PER_STEP = 4
BAND_PAIRS_PER_STEP = 4
STICK_PAIRS_PER_STEP = 4
ONES_ROWS = 16
VMEM_LIMIT = 56 * 1024 * 1024
LOG2_E = math.log2(math.e)

F32 = jnp.float32
BF16 = jnp.bfloat16


def _rms(x, g):
    ms = jnp.mean(x * x, axis=-1, keepdims=True)
    return x * lax.rsqrt(ms + EPS) * g


def _nt_dot(a, b):
    return lax.dot_general(a, b, (((1,), (1,)), ((), ())), preferred_element_type=F32)


def _lane_lo(shape):
    return lax.broadcasted_iota(jnp.int32, shape, len(shape) - 1) < HEAD_DIM


def _norm_proj_kernel(x_ref, g_ref, w_ref, gain_ref, wvt_ref, o_ref, vt_ref, *, norm_chunks):
    u = _rms(x_ref[...], g_ref[...]).astype(BF16)
    tm = u.shape[0]
    lo = _lane_lo((tm, LANES))
    n_cols = o_ref.shape[1]
    for c in range(n_cols // MXU_COLS):
        acc = jnp.dot(u, w_ref[:, c * MXU_COLS:(c + 1) * MXU_COLS], preferred_element_type=F32)
        for half in range(MXU_COLS // LANES):
            col = c * MXU_COLS + half * LANES
            y = acc[:, half * LANES:(half + 1) * LANES]
            if norm_chunks[col // LANES]:
                sq = y * y
                s0 = jnp.sum(jnp.where(lo, sq, 0.0), axis=-1, keepdims=True)
                s1 = jnp.sum(jnp.where(lo, 0.0, sq), axis=-1, keepdims=True)
                ms = jnp.where(lo, s0, s1) * (1.0 / HEAD_DIM)
                y = y * lax.rsqrt(ms + EPS)
            o_ref[:, col:col + LANES] = (y * gain_ref[:, col:col + LANES]).astype(BF16)
    vt = _nt_dot(wvt_ref[...], u).astype(BF16)
    for grp in range(vt_ref.shape[1]):
        for t in range(tm // ATT_BLOCK):
            vt_ref[0, grp, t, :LANES, :] = vt[grp * LANES:(grp + 1) * LANES, t * ATT_BLOCK:(t + 1) * ATT_BLOCK]
            vt_ref[0, grp, t, LANES:, :] = jnp.ones((ONES_ROWS, ATT_BLOCK), BF16)


def _norm_proj(h2, g, w, gain, norm_chunks, w_vt, seq):
    t, d = h2.shape
    n = w.shape[1]
    groups = w_vt.shape[0] // LANES
    tiles_per_seq = seq // ROW_TILE
    tiles_per_step = ROW_TILE // ATT_BLOCK
    v_rows = LANES + ONES_ROWS
    return pl.pallas_call(
        functools.partial(_norm_proj_kernel, norm_chunks=norm_chunks),
        grid=(t // ROW_TILE,),
        in_specs=[
            pl.BlockSpec((ROW_TILE, d), lambda i: (i, 0)),
            pl.BlockSpec((1, d), lambda i: (0, 0)),
            pl.BlockSpec((d, n), lambda i: (0, 0)),
            pl.BlockSpec((1, n), lambda i: (0, 0)),
            pl.BlockSpec(w_vt.shape, lambda i: (0, 0)),
        ],
        out_specs=[
            pl.BlockSpec((ROW_TILE, n), lambda i: (i, 0)),
            pl.BlockSpec((1, groups, tiles_per_step, v_rows, ATT_BLOCK),
                         lambda i: (i // tiles_per_seq, 0, i % tiles_per_seq, 0, 0)),
        ],
        out_shape=[
            jax.ShapeDtypeStruct((t, n), BF16),
            jax.ShapeDtypeStruct((t // seq, groups, seq // ATT_BLOCK, v_rows, ATT_BLOCK), BF16),
        ],
        compiler_params=pltpu.CompilerParams(
            dimension_semantics=("parallel",), vmem_limit_bytes=VMEM_LIMIT),
        name="norm_proj",
    )(h2, g.reshape(1, d), w, gain.reshape(1, n), w_vt)


def _mix_ffn_kernel(*refs, n_y, tiles_per_seq):
    h_ref, halo_ref = refs[:2]
    y_refs, yhalo_refs = refs[2:2 + n_y], refs[2 + n_y:2 + 2 * n_y]
    wout_ref, g_ref, wup_ref, cw_ref, cb_ref, wdn_ref, o_ref, hd_ref = refs[2 + 2 * n_y:]

    def mixed(h, ys):
        row = 0
        for y_ref in ys:
            k = y_ref.shape[1]
            h = h + jnp.dot(y_ref[...], wout_ref[row:row + k, :], preferred_element_type=F32)
            row += k
        return h

    x = mixed(h_ref[...], y_refs)
    g = g_ref[...]
    tm = x.shape[0]
    keep_halo = jnp.where((pl.program_id(0) % tiles_per_seq) != 0, 1.0, 0.0)
    u_halo = _rms(mixed(halo_ref[...], yhalo_refs), g) * keep_halo
    u_ext = jnp.concatenate([u_halo, _rms(x, g)], axis=0).astype(BF16)

    def conv(col):
        hd_ref[...] = jnp.dot(u_ext, wup_ref[:, col:col + D_FF], preferred_element_type=F32)
        out = cb_ref[:, col:col + D_FF]
        for back in range(CONV_WIDTH):
            tap = cw_ref[CONV_WIDTH - 1 - back:CONV_WIDTH - back, col:col + D_FF]
            out = out + hd_ref[pl.ds(CONV_HALO - back, tm), :] * tap
        return out

    gate = conv(0)
    val = conv(D_FF)
    act = (gate * jax.nn.sigmoid(gate) * val).astype(BF16)
    o_ref[...] = x + jnp.dot(act, wdn_ref[...], preferred_element_type=F32)


def _mix_ffn(h2, ys, w_out, g, w_up, conv_w, conv_b, w_down, seq):
    t, d = h2.shape
    halo_blocks = ROW_TILE // CONV_HALO
    const = lambda i: (0, 0)
    tile = lambda i: (i, 0)
    halo = lambda i: (jnp.maximum(i * halo_blocks - 1, 0), 0)
    resident = lambda a: pl.BlockSpec(a.shape, const, pipeline_mode=pl.Buffered(1))
    in_specs = [pl.BlockSpec((ROW_TILE, d), tile), pl.BlockSpec((CONV_HALO, d), halo)]
    in_specs += [pl.BlockSpec((ROW_TILE, y.shape[1]), tile) for y in ys]
    in_specs += [pl.BlockSpec((CONV_HALO, y.shape[1]), halo) for y in ys]
    in_specs += [resident(w_out), pl.BlockSpec((1, d), const), resident(w_up), pl.BlockSpec(conv_w.shape, const),
                 pl.BlockSpec((1, 2 * D_FF), const), resident(w_down)]
    return pl.pallas_call(
        functools.partial(_mix_ffn_kernel, n_y=len(ys), tiles_per_seq=seq // ROW_TILE),
        grid=(t // ROW_TILE,),
        in_specs=in_specs,
        out_specs=pl.BlockSpec((ROW_TILE, d), tile),
        out_shape=jax.ShapeDtypeStruct((t, d), F32),
        scratch_shapes=[pltpu.VMEM((ROW_TILE + CONV_HALO, D_FF), F32)],
        compiler_params=pltpu.CompilerParams(
            dimension_semantics=("parallel",), vmem_limit_bytes=VMEM_LIMIT),
        name="mix_out_ffn",
    )(h2, h2, *ys, *ys, w_out, g.reshape(1, d), w_up, conv_w, conv_b.reshape(1, 2 * D_FF), w_down)


def _split_heads(q):
    lo = _lane_lo(q.shape)
    zero = jnp.zeros_like(q)
    return jnp.concatenate([jnp.where(lo, q, zero), jnp.where(lo, zero, q)], axis=0)


def _tile_ids(shape):
    return (lax.broadcasted_iota(jnp.int32, shape, 0), lax.broadcasted_iota(jnp.int32, shape, 1))


def _diff_attn_kernel(lam_ref, q_ref, k_ref, vt_ref, bias_ref, subln_ref, o_ref,
                      s_ref, cmax_ref, m_ref, acc_ref, *, out_scale):
    blk = ATT_BLOCK
    heads = range(DIFF_HEADS_PER_STEP)
    qi = pl.program_id(2)
    lanes = lambda h: slice(h * LANES, (h + 1) * LANES)
    qq = [_split_heads(q_ref[0, :, lanes(h)]) for h in heads]
    m_ref[...] = jnp.full(m_ref.shape, -jnp.inf, F32)
    acc_ref[...] = jnp.zeros(acc_ref.shape, F32)
    n_far = jnp.maximum(qi - 1, 0)
    both = lambda t: jnp.concatenate([t, t], axis=1)

    def scores_to(slot, kj, finish):
        start = pl.multiple_of(kj * blk, blk)
        for h in heads:
            s = finish(h, _nt_dot(k_ref[0, pl.ds(start, blk), lanes(h)], qq[h]))
            s_ref[h, slot] = s
            cmax_ref[h, slot] = jnp.max(s_ref[h, slot], axis=0, keepdims=True)

    def absorb(slot, kj):
        for h in heads:
            m_prev = m_ref[h]
            m_new = jnp.maximum(m_prev, cmax_ref[h, slot])
            alpha = jnp.exp2(m_prev - m_new)
            p = jnp.exp2(s_ref[h, slot] - m_new).astype(BF16)
            pv = jnp.dot(vt_ref[0, h, kj], p, preferred_element_type=F32)
            acc_ref[h] = alpha * acc_ref[h] + pv
            m_ref[h] = m_new

    def far_key_tile(i):
        return jnp.maximum(qi - 2 - i, 0)

    prev = jnp.maximum(qi - 1, 0)
    no_prev = jnp.where(qi >= 1, 0.0, -jnp.inf)
    scores_to(0, qi, lambda h, s: s + both(bias_ref[h, 0]))
    scores_to(1, prev, lambda h, s: s + both(bias_ref[h, 1] + no_prev))
    absorb(0, qi)
    scores_to(0, far_key_tile(0), lambda h, s: s)
    absorb(1, prev)

    def far_trip(first, finish_for):
        for u in range(FAR_GROUP):
            scores_to((u + 1) % 2, far_key_tile(first + u + 1), finish_for(first + u + 1))
            absorb(u % 2, far_key_tile(first + u))

    full_trips = n_far // FAR_GROUP

    def full_trip(g, carry):
        far_trip(g * FAR_GROUP, lambda i: (lambda h, s: s))
        return carry

    lax.fori_loop(0, full_trips, full_trip, 0)

    @pl.when(n_far % FAR_GROUP != 0)
    def _():
        far_trip(full_trips * FAR_GROUP, lambda i: (lambda h, s: s + jnp.where(i < n_far, 0.0, -jnp.inf)))

    for h in heads:
        acc = acc_ref[h]
        o = acc[:LANES] / acc[LANES:LANES + 1]
        ya = o[:, :blk] - lam_ref[0, 0] * o[:, blk:]
        ms = jnp.mean(ya * ya, axis=0, keepdims=True)
        ya = ya * lax.rsqrt(ms + EPS) * (subln_ref[...] * out_scale)
        o_ref[0, :, lanes(h)] = ya.T.astype(BF16)


def _diff_attn(proj, vt, bias_tab, subln, lam, out_scale):
    b, s, _ = proj.shape
    nq = s // ATT_BLOCK
    hps = DIFF_HEADS_PER_STEP
    width = hps * LANES
    q_blk, k_blk = 0, A_WIDTH // width
    v_rows = LANES + ONES_ROWS
    return pl.pallas_call(
        functools.partial(_diff_attn_kernel, out_scale=out_scale),
        grid=(b, A_HEADS // hps, nq),
        in_specs=[
            pl.BlockSpec(memory_space=pltpu.SMEM),
            pl.BlockSpec((1, ATT_BLOCK, width), lambda bi, g, qi: (bi, qi, q_blk + g)),
            pl.BlockSpec((1, s, width), lambda bi, g, qi: (bi, 0, k_blk + g)),
            pl.BlockSpec((1, hps, nq, v_rows, ATT_BLOCK), lambda bi, g, qi: (bi, g, 0, 0, 0)),
            pl.BlockSpec((hps, 2, ATT_BLOCK, ATT_BLOCK), lambda bi, g, qi: (g, 0, 0, 0)),
            pl.BlockSpec((LANES, 1), lambda bi, g, qi: (0, 0)),
        ],
        out_specs=pl.BlockSpec((1, ATT_BLOCK, width), lambda bi, g, qi: (bi, qi, g)),
        out_shape=jax.ShapeDtypeStruct((b, s, A_WIDTH), BF16),
        scratch_shapes=[
            pltpu.VMEM((hps, 2, ATT_BLOCK, 2 * ATT_BLOCK), F32),
            pltpu.VMEM((hps, 2, 1, 2 * ATT_BLOCK), F32),
            pltpu.VMEM((hps, 1, 2 * ATT_BLOCK), F32),
            pltpu.VMEM((hps, v_rows, 2 * ATT_BLOCK), F32),
        ],
        compiler_params=pltpu.CompilerParams(
            dimension_semantics=("parallel", "parallel", "arbitrary"), vmem_limit_bytes=VMEM_LIMIT),
        name="diff_attn",
    )(lam.reshape(1, 1), proj, proj, vt, bias_tab, subln.reshape(LANES, 1))


STICK_DEAD_LOGIT = -104.0


def _stick_kernel(q_ref, k_ref, vt_ref, o_ref, carry_ref, acc_ref):
    blk = ATT_BLOCK
    pairs = range(STICK_PAIRS_PER_STEP)
    qi = pl.program_id(2)
    lanes = lambda g: slice(g * LANES, (g + 1) * LANES)
    qq = [_split_heads(q_ref[0, :, lanes(g)]) for g in pairs]
    carry_ref[...] = jnp.zeros(carry_ref.shape, F32)
    acc_ref[...] = jnp.zeros(acc_ref.shape, F32)
    kk, qc = _tile_ids((blk, blk))
    later = jnp.where(qc > kk, 1.0, 0.0).astype(BF16)
    causal = jnp.concatenate([kk < qc, kk < qc], axis=1)

    def update(kj, diagonal):
        start = pl.multiple_of(kj * blk, blk)
        logits = []
        for g in pairs:
            z = _nt_dot(k_ref[0, pl.ds(start, blk), lanes(g)], qq[g])
            log_1m = -(jnp.maximum(z, 0.0) + jnp.log(1.0 + jnp.exp(-jnp.abs(z))))
            if diagonal:
                log_1m = jnp.where(causal, log_1m, 0.0)
            hi = log_1m.astype(BF16)
            lo = (log_1m - hi.astype(F32)).astype(BF16)
            remain = (jnp.dot(later, hi, preferred_element_type=F32)
                      + jnp.dot(later, lo, preferred_element_type=F32))
            carry = carry_ref[g]
            logits.append(z + log_1m + remain + carry)
            carry_ref[g] = carry + (remain[:1] + log_1m[:1])
        for g in pairs:
            a = jnp.exp(logits[g])
            if diagonal:
                a = jnp.where(causal, a, 0.0)
            acc_ref[g] += jnp.dot(vt_ref[0, g, kj, :LANES, :], a.astype(BF16), preferred_element_type=F32)

    def live():
        return jnp.max(carry_ref[...]) > STICK_DEAD_LOGIT

    update(qi, True)

    def body(state):
        j, _ = state
        update(qi - 1 - j, False)
        return j + 1, live()

    lax.while_loop(lambda st: jnp.logical_and(st[0] < qi, st[1]), body, (jnp.int32(0), live()))

    for g in pairs:
        acc = acc_ref[g]
        o = jnp.concatenate([acc[:HEAD_DIM, :blk], acc[HEAD_DIM:, blk:]], axis=0)
        o_ref[0, :, lanes(g)] = o.T.astype(BF16)


def _stick_attn(proj, vt):
    b, s, _ = proj.shape
    nq = s // ATT_BLOCK
    pps = STICK_PAIRS_PER_STEP
    width = pps * LANES
    base = 2 * A_WIDTH // width
    q_blk, k_blk = base, base + B_WIDTH // width
    vt_blk = A_HEADS // pps
    return pl.pallas_call(
        _stick_kernel,
        grid=(b, B_WIDTH // width, nq),
        in_specs=[
            pl.BlockSpec((1, ATT_BLOCK, width), lambda bi, g, qi: (bi, qi, q_blk + g)),
            pl.BlockSpec((1, s, width), lambda bi, g, qi: (bi, 0, k_blk + g)),
            pl.BlockSpec((1, pps, nq, LANES + ONES_ROWS, ATT_BLOCK), lambda bi, g, qi: (bi, vt_blk + g, 0, 0, 0)),
        ],
        out_specs=pl.BlockSpec((1, ATT_BLOCK, width), lambda bi, g, qi: (bi, qi, g)),
        out_shape=jax.ShapeDtypeStruct((b, s, B_WIDTH), BF16),
        scratch_shapes=[
            pltpu.VMEM((pps, 1, 2 * ATT_BLOCK), F32),
            pltpu.VMEM((pps, LANES, 2 * ATT_BLOCK), F32),
        ],
        compiler_params=pltpu.CompilerParams(
            dimension_semantics=("parallel", "parallel", "arbitrary"), vmem_limit_bytes=VMEM_LIMIT),
        name="stick_breaking",
    )(proj, proj, vt)


def _band_kernel(q_ref, k_ref, vt_ref, bias_ref, o_ref, s_ref):
    blk = ATT_BLOCK
    qi = pl.program_id(2)
    key_tiles = [jnp.maximum(qi - d, 0) for d in range(3)]
    padding = [None] + [jnp.where(qi >= d, 0.0, -jnp.inf) for d in (1, 2)]

    def score(g):
        lanes = slice(g * LANES, (g + 1) * LANES)
        qq = _split_heads(q_ref[0, :, lanes])
        tile_max = []
        for d in range(3):
            start = pl.multiple_of(key_tiles[d] * blk, blk)
            s = _nt_dot(k_ref[0, pl.ds(start, blk), lanes], qq)
            bias = jnp.concatenate([bias_ref[2 * g, d], bias_ref[2 * g + 1, d]], axis=1)
            s = s + (bias if padding[d] is None else bias + padding[d])
            s_ref[g, d] = s
            tile_max.append(jnp.max(s_ref[g, d], axis=0, keepdims=True))
        return functools.reduce(jnp.maximum, tile_max)

    def attend(g, m):
        lanes = slice(g * LANES, (g + 1) * LANES)
        acc = jnp.zeros((LANES + ONES_ROWS, 2 * blk), F32)
        for d in range(3):
            p = jnp.exp2(s_ref[g, d] - m).astype(BF16)
            acc = acc + jnp.dot(vt_ref[0, g, key_tiles[d]], p, preferred_element_type=F32)
        o = acc[:LANES] / acc[LANES:LANES + 1]
        o = jnp.concatenate([o[:HEAD_DIM, :blk], o[HEAD_DIM:, blk:]], axis=0)
        o_ref[0, :, lanes] = o.T.astype(BF16)

    pending = score(0)
    for g in range(BAND_PAIRS_PER_STEP):
        following = score(g + 1) if g + 1 < BAND_PAIRS_PER_STEP else None
        attend(g, pending)
        pending = following


def _band_attn(proj, vt, bias_tab):
    b, s, _ = proj.shape
    nq = s // ATT_BLOCK
    pps = BAND_PAIRS_PER_STEP
    width = pps * LANES
    q_blk, k_blk = 0, C_WIDTH // width
    return pl.pallas_call(
        _band_kernel,
        grid=(b, C_WIDTH // width, nq),
        in_specs=[
            pl.BlockSpec((1, ATT_BLOCK, width), lambda bi, g, qi: (bi, qi, q_blk + g)),
            pl.BlockSpec((1, s, width), lambda bi, g, qi: (bi, 0, k_blk + g)),
            pl.BlockSpec((1, pps, nq, LANES + ONES_ROWS, ATT_BLOCK), lambda bi, g, qi: (bi, g, 0, 0, 0)),
            pl.BlockSpec((2 * pps, 3, ATT_BLOCK, ATT_BLOCK), lambda bi, g, qi: (g, 0, 0, 0)),
        ],
        out_specs=pl.BlockSpec((1, ATT_BLOCK, width), lambda bi, g, qi: (bi, qi, g)),
        out_shape=jax.ShapeDtypeStruct((b, s, C_WIDTH), BF16),
        scratch_shapes=[pltpu.VMEM((pps, 3, ATT_BLOCK, 2 * ATT_BLOCK), F32)],
        compiler_params=pltpu.CompilerParams(
            dimension_semantics=("parallel", "parallel", "arbitrary"), vmem_limit_bytes=VMEM_LIMIT),
        name="chunk_band",
    )(proj, proj, vt, bias_tab)


def _t5_bucket(rel):
    nb = T5_BUCKETS // 2
    max_exact = nb // 2
    n = jnp.abs(rel)
    large = max_exact + (jnp.log(jnp.maximum(n, 1).astype(jnp.float32) / max_exact)
                         / math.log(T5_MAX_DIST / max_exact) * (nb - max_exact)).astype(jnp.int32)
    large = jnp.minimum(large, nb - 1)
    return jnp.where(rel > 0, nb, 0) + jnp.where(n < max_exact, n, large)


def _toeplitz_tiles(per_distance):
    blk = ATT_BLOCK
    vec = per_distance.T.astype(F32)
    h, period = vec.shape
    skew = jnp.tile(vec, (1, blk))[:, :blk * (period - 1)].reshape(h, blk, period - 1)
    return jnp.stack([skew[:, :, blk - 1 + blk * d:2 * blk - 1 + blk * d] for d in range(3)], axis=1)


def _tile_distances():
    return jnp.arange(4 * ATT_BLOCK) - (ATT_BLOCK - 1)


def _tile_chunks():
    i = jnp.arange(ATT_BLOCK) // CHUNK
    return i[:, None], i[None, :]


def _t5_bias_tiles(t5_bias):
    assert ATT_BLOCK >= T5_MAX_DIST
    tiles = _toeplitz_tiles(t5_bias[_t5_bucket(-_tile_distances())])
    far = tiles[:, 2, :1, :1]
    tiles = (tiles[:, :2] - far[:, None]) * LOG2_E
    key_chunk, query_chunk = _tile_chunks()
    return tiles.at[:, 0].add(jnp.where(key_chunk <= query_chunk, 0.0, -jnp.inf))


def _band_bias_tiles(rel_bias):
    assert C_LEFT_CHUNKS * CHUNK == 2 * ATT_BLOCK
    tiles = _toeplitz_tiles(rel_bias[jnp.clip(_tile_distances(), -C_MAX_REL, C_MAX_REL) + C_MAX_REL]) * LOG2_E
    key_chunk, query_chunk = _tile_chunks()
    tiles = tiles.at[:, 0].add(jnp.where(key_chunk <= query_chunk, 0.0, -jnp.inf))
    return tiles.at[:, 2].add(jnp.where(key_chunk >= query_chunk, 0.0, -jnp.inf))


def _chunk_flags(*sections):
    flags = []
    for width, normed in sections:
        flags += [normed] * (width // LANES)
    return tuple(flags)


def kernel(x, t5_bias, norm_mix, norm_ffn, ab_w_in, a_q_norm, a_k_norm, a_lambda, a_subln, ab_w_out,
           c_w_in, c_q_norm, c_k_norm, c_rel_bias, c_w_out, ffn_w_up, ffn_conv_w, ffn_conv_b, ffn_w_down):
    b, seq, d = x.shape
    depth = norm_mix.shape[0]
    assert d == D_MODEL and seq % ROW_TILE == 0 and seq % ATT_BLOCK == 0
    scale = HEAD_DIM ** -0.5
    h = x.reshape(b * seq, d)
    t5_tiles = _t5_bias_tiles(t5_bias)
    ones = lambda n: jnp.ones((n,), F32)
    even_flags = _chunk_flags((2 * A_WIDTH, True), (2 * B_WIDTH, False))
    odd_flags = _chunk_flags((2 * C_WIDTH, True))
    for layer in range(depth):
        if layer % 2 == 0:
            e = layer // 2
            lam_init = 0.8 - 0.6 * math.exp(-0.3 * layer)
            w_in = ab_w_in[e].astype(BF16)
            gain = jnp.concatenate([
                jnp.tile(a_q_norm[e], 2 * A_HEADS) * (scale * LOG2_E), jnp.tile(a_k_norm[e], 2 * A_HEADS),
                ones(B_WIDTH) * scale, ones(B_WIDTH)])
            qk_cols = 2 * A_WIDTH
            w_main = jnp.concatenate([w_in[:, :qk_cols], w_in[:, 3 * A_WIDTH:3 * A_WIDTH + 2 * B_WIDTH]], axis=1)
            w_v = jnp.concatenate([w_in[:, qk_cols:3 * A_WIDTH], w_in[:, 3 * A_WIDTH + 2 * B_WIDTH:]], axis=1)
            proj, vt = _norm_proj(h, norm_mix[layer], w_main, gain, even_flags, w_v.T, seq)
            proj = proj.reshape(b, seq, -1)
            lam_p = a_lambda[e]
            lam = jnp.exp(jnp.sum(lam_p[0] * lam_p[1])) - jnp.exp(jnp.sum(lam_p[2] * lam_p[3])) + lam_init
            ya = _diff_attn(proj, vt, t5_tiles, a_subln[e], lam.astype(F32), 1.0 - lam_init)
            yb = _stick_attn(proj, vt)
            ys = [ya.reshape(b * seq, A_WIDTH), yb.reshape(b * seq, B_WIDTH)]
            w_out = ab_w_out[e]
        else:
            o = layer // 2
            w_in = c_w_in[o].astype(BF16)
            gain = jnp.concatenate([
                jnp.tile(c_q_norm[o], C_HEADS) * (scale * LOG2_E), jnp.tile(c_k_norm[o], C_HEADS)])
            proj, vt = _norm_proj(h, norm_mix[layer], w_in[:, :2 * C_WIDTH], gain, odd_flags,
                                  w_in[:, 2 * C_WIDTH:].T, seq)
            proj = proj.reshape(b, seq, -1)
            y = _band_attn(proj, vt, _band_bias_tiles(c_rel_bias[o]))
            ys = [y.reshape(b * seq, C_WIDTH)]
            w_out = c_w_out[o]
        h = _mix_ffn(h, ys, w_out.astype(BF16), norm_ffn[layer], ffn_w_up[layer].astype(BF16), ffn_conv_w[layer],
                     ffn_conv_b[layer], ffn_w_down[layer].astype(BF16), seq)
    return h.reshape(b, seq, d)
```

```python
import functools
import math

import jax
import jax.numpy as jnp
from jax import lax
from jax.experimental import pallas as pl
from jax.experimental.pallas import tpu as pltpu

D_MODEL = 1024
HEAD_DIM = 64
CHUNK = 64
A_HEADS = 4
B_HEADS = 8
C_HEADS = 16
A_WIDTH = A_HEADS * 2 * HEAD_DIM
B_WIDTH = B_HEADS * HEAD_DIM
C_WIDTH = C_HEADS * HEAD_DIM
T5_BUCKETS = 32
T5_MAX_DIST = 128
C_LEFT_CHUNKS = 8
C_MAX_REL = 128
D_FF = 2816
CONV_WIDTH = 3
EPS = 1e-6

LANES = 128
MXU_COLS = 256
ROW_TILE = 512
ATT_BLOCK = 256
CONV_HALO = 16
FAR_GROUP = 4
DIFF_HEADS_PER_STEP = 4
BAND_PAIRS_PER_STEP = 4
STICK_PAIRS_PER_STEP = 4
ONES_ROWS = 16
VMEM_LIMIT = 56 * 1024 * 1024
LOG2_E = math.log2(math.e)

F32 = jnp.float32
BF16 = jnp.bfloat16


def _rms(x, g):
    ms = jnp.mean(x * x, axis=-1, keepdims=True)
    return x * lax.rsqrt(ms + EPS) * g


def _nt_dot(a, b):
    return lax.dot_general(a, b, (((1,), (1,)), ((), ())), preferred_element_type=F32)


def _lane_lo(shape):
    return lax.broadcasted_iota(jnp.int32, shape, len(shape) - 1) < HEAD_DIM


def _norm_proj_kernel(x_ref, g_ref, w_ref, gain_ref, wvt_ref, o_ref, vt_ref, *, norm_chunks):
    u = _rms(x_ref[...], g_ref[...]).astype(BF16)
    tm = u.shape[0]
    lo = _lane_lo((tm, LANES))
    n_cols = o_ref.shape[1]
    for c in range(n_cols // MXU_COLS):
        acc = jnp.dot(u, w_ref[:, c * MXU_COLS:(c + 1) * MXU_COLS], preferred_element_type=F32)
        for half in range(MXU_COLS // LANES):
            col = c * MXU_COLS + half * LANES
            y = acc[:, half * LANES:(half + 1) * LANES]
            if norm_chunks[col // LANES]:
                sq = y * y
                s0 = jnp.sum(jnp.where(lo, sq, 0.0), axis=-1, keepdims=True)
                s1 = jnp.sum(jnp.where(lo, 0.0, sq), axis=-1, keepdims=True)
                ms = jnp.where(lo, s0, s1) * (1.0 / HEAD_DIM)
                y = y * lax.rsqrt(ms + EPS)
            o_ref[:, col:col + LANES] = (y * gain_ref[:, col:col + LANES]).astype(BF16)
    vt = _nt_dot(wvt_ref[...], u).astype(BF16)
    for grp in range(vt_ref.shape[1]):
        for t in range(tm // ATT_BLOCK):
            vt_ref[0, grp, t, :LANES, :] = vt[grp * LANES:(grp + 1) * LANES, t * ATT_BLOCK:(t + 1) * ATT_BLOCK]
            vt_ref[0, grp, t, LANES:, :] = jnp.ones((ONES_ROWS, ATT_BLOCK), BF16)


def _norm_proj(h2, g, w, gain, norm_chunks, w_vt, seq):
    t, d = h2.shape
    n = w.shape[1]
    groups = w_vt.shape[0] // LANES
    tiles_per_seq = seq // ROW_TILE
    tiles_per_step = ROW_TILE // ATT_BLOCK
    v_rows = LANES + ONES_ROWS
    return pl.pallas_call(
        functools.partial(_norm_proj_kernel, norm_chunks=norm_chunks),
        grid=(t // ROW_TILE,),
        in_specs=[
            pl.BlockSpec((ROW_TILE, d), lambda i: (i, 0)),
            pl.BlockSpec((1, d), lambda i: (0, 0)),
            pl.BlockSpec((d, n), lambda i: (0, 0)),
            pl.BlockSpec((1, n), lambda i: (0, 0)),
            pl.BlockSpec(w_vt.shape, lambda i: (0, 0)),
        ],
        out_specs=[
            pl.BlockSpec((ROW_TILE, n), lambda i: (i, 0)),
            pl.BlockSpec((1, groups, tiles_per_step, v_rows, ATT_BLOCK),
                         lambda i: (i // tiles_per_seq, 0, i % tiles_per_seq, 0, 0)),
        ],
        out_shape=[
            jax.ShapeDtypeStruct((t, n), BF16),
            jax.ShapeDtypeStruct((t // seq, groups, seq // ATT_BLOCK, v_rows, ATT_BLOCK), BF16),
        ],
        compiler_params=pltpu.CompilerParams(
            dimension_semantics=("parallel",), vmem_limit_bytes=VMEM_LIMIT),
        name="norm_proj",
    )(h2, g.reshape(1, d), w, gain.reshape(1, n), w_vt)


def _mix_ffn_kernel(*refs, n_y, tiles_per_seq):
    h_ref, halo_ref = refs[:2]
    y_refs, yhalo_refs = refs[2:2 + n_y], refs[2 + n_y:2 + 2 * n_y]
    wout_ref, g_ref, wup_ref, cw_ref, cb_ref, wdn_ref, o_ref, hd_ref = refs[2 + 2 * n_y:]

    def mixed(h, ys):
        row = 0
        for y_ref in ys:
            k = y_ref.shape[1]
            h = h + jnp.dot(y_ref[...], wout_ref[row:row + k, :], preferred_element_type=F32)
            row += k
        return h

    x = mixed(h_ref[...], y_refs)
    g = g_ref[...]
    tm = x.shape[0]
    keep_halo = jnp.where((pl.program_id(0) % tiles_per_seq) != 0, 1.0, 0.0)
    u_halo = _rms(mixed(halo_ref[...], yhalo_refs), g) * keep_halo
    u_ext = jnp.concatenate([u_halo, _rms(x, g)], axis=0).astype(BF16)

    def conv(col):
        hd_ref[...] = jnp.dot(u_ext, wup_ref[:, col:col + D_FF], preferred_element_type=F32)
        out = cb_ref[:, col:col + D_FF]
        for back in range(CONV_WIDTH):
            tap = cw_ref[CONV_WIDTH - 1 - back:CONV_WIDTH - back, col:col + D_FF]
            out = out + hd_ref[pl.ds(CONV_HALO - back, tm), :] * tap
        return out

    gate = conv(0)
    val = conv(D_FF)
    act = (gate * jax.nn.sigmoid(gate) * val).astype(BF16)
    o_ref[...] = x + jnp.dot(act, wdn_ref[...], preferred_element_type=F32)


def _mix_ffn(h2, ys, w_out, g, w_up, conv_w, conv_b, w_down, seq):
    t, d = h2.shape
    halo_blocks = ROW_TILE // CONV_HALO
    const = lambda i: (0, 0)
    tile = lambda i: (i, 0)
    halo = lambda i: (jnp.maximum(i * halo_blocks - 1, 0), 0)
    resident = lambda a: pl.BlockSpec(a.shape, const, pipeline_mode=pl.Buffered(1))
    in_specs = [pl.BlockSpec((ROW_TILE, d), tile), pl.BlockSpec((CONV_HALO, d), halo)]
    in_specs += [pl.BlockSpec((ROW_TILE, y.shape[1]), tile) for y in ys]
    in_specs += [pl.BlockSpec((CONV_HALO, y.shape[1]), halo) for y in ys]
    in_specs += [resident(w_out), pl.BlockSpec((1, d), const), resident(w_up), pl.BlockSpec(conv_w.shape, const),
                 pl.BlockSpec((1, 2 * D_FF), const), resident(w_down)]
    return pl.pallas_call(
        functools.partial(_mix_ffn_kernel, n_y=len(ys), tiles_per_seq=seq // ROW_TILE),
        grid=(t // ROW_TILE,),
        in_specs=in_specs,
        out_specs=pl.BlockSpec((ROW_TILE, d), tile),
        out_shape=jax.ShapeDtypeStruct((t, d), F32),
        scratch_shapes=[pltpu.VMEM((ROW_TILE + CONV_HALO, D_FF), F32)],
        compiler_params=pltpu.CompilerParams(
            dimension_semantics=("parallel",), vmem_limit_bytes=VMEM_LIMIT),
        name="mix_out_ffn",
    )(h2, h2, *ys, *ys, w_out, g.reshape(1, d), w_up, conv_w, conv_b.reshape(1, 2 * D_FF), w_down)


def _split_heads(q):
    lo = _lane_lo(q.shape)
    zero = jnp.zeros_like(q)
    return jnp.concatenate([jnp.where(lo, q, zero), jnp.where(lo, zero, q)], axis=0)


def _tile_ids(shape):
    return (lax.broadcasted_iota(jnp.int32, shape, 0), lax.broadcasted_iota(jnp.int32, shape, 1))


def _diff_attn_kernel(lam_ref, q_ref, k_ref, vt_ref, bias_ref, subln_ref, o_ref,
                      s_ref, cmax_ref, m_ref, acc_ref, *, out_scale):
    blk = ATT_BLOCK
    heads = range(DIFF_HEADS_PER_STEP)
    qi = pl.program_id(2)
    lanes = lambda h: slice(h * LANES, (h + 1) * LANES)
    qq = [_split_heads(q_ref[0, :, lanes(h)]) for h in heads]
    m_ref[...] = jnp.full(m_ref.shape, -jnp.inf, F32)
    acc_ref[...] = jnp.zeros(acc_ref.shape, F32)
    n_far = jnp.maximum(qi - 1, 0)
    both = lambda t: jnp.concatenate([t, t], axis=1)

    def scores_to(slot, kj, finish):
        start = pl.multiple_of(kj * blk, blk)
        for h in heads:
            s = finish(h, _nt_dot(k_ref[0, pl.ds(start, blk), lanes(h)], qq[h]))
            s_ref[h, slot] = s
            cmax_ref[h, slot] = jnp.max(s_ref[h, slot], axis=0, keepdims=True)

    def absorb(slot, kj):
        for h in heads:
            m_prev = m_ref[h]
            m_new = jnp.maximum(m_prev, cmax_ref[h, slot])
            alpha = jnp.exp2(m_prev - m_new)
            p = jnp.exp2(s_ref[h, slot] - m_new).astype(BF16)
            pv = jnp.dot(vt_ref[0, h, kj], p, preferred_element_type=F32)
            acc_ref[h] = alpha * acc_ref[h] + pv
            m_ref[h] = m_new

    def far_key_tile(i):
        return jnp.maximum(qi - 2 - i, 0)

    prev = jnp.maximum(qi - 1, 0)
    no_prev = jnp.where(qi >= 1, 0.0, -jnp.inf)
    scores_to(0, qi, lambda h, s: s + both(bias_ref[h, 0]))
    scores_to(1, prev, lambda h, s: s + both(bias_ref[h, 1] + no_prev))
    absorb(0, qi)
    scores_to(0, far_key_tile(0), lambda h, s: s)
    absorb(1, prev)

    def far_steps(first, steps, finish_for):
        for u in range(steps):
            scores_to((u + 1) % 2, far_key_tile(first + u + 1), finish_for(first + u + 1))
            absorb(u % 2, far_key_tile(first + u))

    full_trips = n_far // FAR_GROUP

    def full_trip(g, carry):
        far_steps(g * FAR_GROUP, FAR_GROUP, lambda i: (lambda h, s: s))
        return carry

    lax.fori_loop(0, full_trips, full_trip, 0)

    done = full_trips * FAR_GROUP

    def tail_trip(t, carry):
        far_steps(done + 2 * t, 2, lambda i: (lambda h, s: s + jnp.where(i < n_far, 0.0, -jnp.inf)))
        return carry

    lax.fori_loop(0, (n_far - done + 1) // 2, tail_trip, 0)

    for h in heads:
        acc = acc_ref[h]
        o = acc[:LANES] / acc[LANES:LANES + 1]
        ya = o[:, :blk] - lam_ref[0, 0] * o[:, blk:]
        ms = jnp.mean(ya * ya, axis=0, keepdims=True)
        ya = ya * lax.rsqrt(ms + EPS) * (subln_ref[...] * out_scale)
        o_ref[0, :, lanes(h)] = ya.T.astype(BF16)


def _diff_attn(proj, vt, bias_tab, subln, lam, out_scale):
    b, s, _ = proj.shape
    nq = s // ATT_BLOCK
    hps = DIFF_HEADS_PER_STEP
    width = hps * LANES
    q_blk, k_blk = 0, A_WIDTH // width
    v_rows = LANES + ONES_ROWS
    return pl.pallas_call(
        functools.partial(_diff_attn_kernel, out_scale=out_scale),
        grid=(b, A_HEADS // hps, nq),
        in_specs=[
            pl.BlockSpec(memory_space=pltpu.SMEM),
            pl.BlockSpec((1, ATT_BLOCK, width), lambda bi, g, qi: (bi, qi, q_blk + g)),
            pl.BlockSpec((1, s, width), lambda bi, g, qi: (bi, 0, k_blk + g)),
            pl.BlockSpec((1, hps, nq, v_rows, ATT_BLOCK), lambda bi, g, qi: (bi, g, 0, 0, 0)),
            pl.BlockSpec((hps, 2, ATT_BLOCK, ATT_BLOCK), lambda bi, g, qi: (g, 0, 0, 0)),
            pl.BlockSpec((LANES, 1), lambda bi, g, qi: (0, 0)),
        ],
        out_specs=pl.BlockSpec((1, ATT_BLOCK, width), lambda bi, g, qi: (bi, qi, g)),
        out_shape=jax.ShapeDtypeStruct((b, s, A_WIDTH), BF16),
        scratch_shapes=[
            pltpu.VMEM((hps, 2, ATT_BLOCK, 2 * ATT_BLOCK), F32),
            pltpu.VMEM((hps, 2, 1, 2 * ATT_BLOCK), F32),
            pltpu.VMEM((hps, 1, 2 * ATT_BLOCK), F32),
            pltpu.VMEM((hps, v_rows, 2 * ATT_BLOCK), F32),
        ],
        compiler_params=pltpu.CompilerParams(
            dimension_semantics=("parallel", "parallel", "arbitrary"), vmem_limit_bytes=VMEM_LIMIT),
        name="diff_attn",
    )(lam.reshape(1, 1), proj, proj, vt, bias_tab, subln.reshape(LANES, 1))


STICK_DEAD_LOGIT = -104.0


def _stick_kernel(q_ref, k_ref, vt_ref, o_ref, carry_ref, acc_ref):
    blk = ATT_BLOCK
    pairs = range(STICK_PAIRS_PER_STEP)
    qi = pl.program_id(2)
    lanes = lambda g: slice(g * LANES, (g + 1) * LANES)
    qq = [_split_heads(q_ref[0, :, lanes(g)]) for g in pairs]
    carry_ref[...] = jnp.zeros(carry_ref.shape, F32)
    acc_ref[...] = jnp.zeros(acc_ref.shape, F32)
    kk, qc = _tile_ids((blk, blk))
    later = jnp.where(qc > kk, 1.0, 0.0).astype(BF16)
    causal = jnp.concatenate([kk < qc, kk < qc], axis=1)

    def update(kj, diagonal):
        start = pl.multiple_of(kj * blk, blk)
        logits = []
        for g in pairs:
            z = _nt_dot(k_ref[0, pl.ds(start, blk), lanes(g)], qq[g])
            log_1m = -(jnp.maximum(z, 0.0) + jnp.log(1.0 + jnp.exp(-jnp.abs(z))))
            if diagonal:
                log_1m = jnp.where(causal, log_1m, 0.0)
            hi = log_1m.astype(BF16)
            lo = (log_1m - hi.astype(F32)).astype(BF16)
            remain = (jnp.dot(later, hi, preferred_element_type=F32)
                      + jnp.dot(later, lo, preferred_element_type=F32))
            carry = carry_ref[g]
            logits.append(z + log_1m + remain + carry)
            carry_ref[g] = carry + (remain[:1] + log_1m[:1])
        for g in pairs:
            a = jnp.exp(logits[g])
            if diagonal:
                a = jnp.where(causal, a, 0.0)
            acc_ref[g] += jnp.dot(vt_ref[0, g, kj, :LANES, :], a.astype(BF16), preferred_element_type=F32)

    def live():
        return jnp.max(carry_ref[...]) > STICK_DEAD_LOGIT

    update(qi, True)

    def body(state):
        j, _ = state
        update(qi - 1 - j, False)
        return j + 1, live()

    lax.while_loop(lambda st: jnp.logical_and(st[0] < qi, st[1]), body, (jnp.int32(0), live()))

    for g in pairs:
        acc = acc_ref[g]
        o = jnp.concatenate([acc[:HEAD_DIM, :blk], acc[HEAD_DIM:, blk:]], axis=0)
        o_ref[0, :, lanes(g)] = o.T.astype(BF16)


def _stick_attn(proj, vt):
    b, s, _ = proj.shape
    nq = s // ATT_BLOCK
    pps = STICK_PAIRS_PER_STEP
    width = pps * LANES
    base = 2 * A_WIDTH // width
    q_blk, k_blk = base, base + B_WIDTH // width
    vt_blk = A_HEADS // pps
    return pl.pallas_call(
        _stick_kernel,
        grid=(b, B_WIDTH // width, nq),
        in_specs=[
            pl.BlockSpec((1, ATT_BLOCK, width), lambda bi, g, qi: (bi, qi, q_blk + g)),
            pl.BlockSpec((1, s, width), lambda bi, g, qi: (bi, 0, k_blk + g)),
            pl.BlockSpec((1, pps, nq, LANES + ONES_ROWS, ATT_BLOCK), lambda bi, g, qi: (bi, vt_blk + g, 0, 0, 0)),
        ],
        out_specs=pl.BlockSpec((1, ATT_BLOCK, width), lambda bi, g, qi: (bi, qi, g)),
        out_shape=jax.ShapeDtypeStruct((b, s, B_WIDTH), BF16),
        scratch_shapes=[
            pltpu.VMEM((pps, 1, 2 * ATT_BLOCK), F32),
            pltpu.VMEM((pps, LANES, 2 * ATT_BLOCK), F32),
        ],
        compiler_params=pltpu.CompilerParams(
            dimension_semantics=("parallel", "parallel", "arbitrary"), vmem_limit_bytes=VMEM_LIMIT),
        name="stick_breaking",
    )(proj, proj, vt)


def _band_kernel(q_ref, k_ref, vt_ref, bias_ref, o_ref, s_ref):
    blk = ATT_BLOCK
    qi = pl.program_id(2)
    key_tiles = [jnp.maximum(qi - d, 0) for d in range(3)]
    padding = [None] + [jnp.where(qi >= d, 0.0, -jnp.inf) for d in (1, 2)]

    def score(g):
        lanes = slice(g * LANES, (g + 1) * LANES)
        qq = _split_heads(q_ref[0, :, lanes])
        tile_max = []
        for d in range(3):
            start = pl.multiple_of(key_tiles[d] * blk, blk)
            s = _nt_dot(k_ref[0, pl.ds(start, blk), lanes], qq)
            bias = jnp.concatenate([bias_ref[2 * g, d], bias_ref[2 * g + 1, d]], axis=1)
            s = s + (bias if padding[d] is None else bias + padding[d])
            s_ref[g, d] = s
            tile_max.append(jnp.max(s_ref[g, d], axis=0, keepdims=True))
        return functools.reduce(jnp.maximum, tile_max)

    def attend(g, m):
        lanes = slice(g * LANES, (g + 1) * LANES)
        acc = jnp.zeros((LANES + ONES_ROWS, 2 * blk), F32)
        for d in range(3):
            p = jnp.exp2(s_ref[g, d] - m).astype(BF16)
            acc = acc + jnp.dot(vt_ref[0, g, key_tiles[d]], p, preferred_element_type=F32)
        o = acc[:LANES] / acc[LANES:LANES + 1]
        o = jnp.concatenate([o[:HEAD_DIM, :blk], o[HEAD_DIM:, blk:]], axis=0)
        o_ref[0, :, lanes] = o.T.astype(BF16)

    pending = score(0)
    for g in range(BAND_PAIRS_PER_STEP):
        following = score(g + 1) if g + 1 < BAND_PAIRS_PER_STEP else None
        attend(g, pending)
        pending = following


def _band_attn(proj, vt, bias_tab):
    b, s, _ = proj.shape
    nq = s // ATT_BLOCK
    pps = BAND_PAIRS_PER_STEP
    width = pps * LANES
    q_blk, k_blk = 0, C_WIDTH // width
    return pl.pallas_call(
        _band_kernel,
        grid=(b, C_WIDTH // width, nq),
        in_specs=[
            pl.BlockSpec((1, ATT_BLOCK, width), lambda bi, g, qi: (bi, qi, q_blk + g)),
            pl.BlockSpec((1, s, width), lambda bi, g, qi: (bi, 0, k_blk + g)),
            pl.BlockSpec((1, pps, nq, LANES + ONES_ROWS, ATT_BLOCK), lambda bi, g, qi: (bi, g, 0, 0, 0)),
            pl.BlockSpec((2 * pps, 3, ATT_BLOCK, ATT_BLOCK), lambda bi, g, qi: (g, 0, 0, 0)),
        ],
        out_specs=pl.BlockSpec((1, ATT_BLOCK, width), lambda bi, g, qi: (bi, qi, g)),
        out_shape=jax.ShapeDtypeStruct((b, s, C_WIDTH), BF16),
        scratch_shapes=[pltpu.VMEM((pps, 3, ATT_BLOCK, 2 * ATT_BLOCK), F32)],
        compiler_params=pltpu.CompilerParams(
            dimension_semantics=("parallel", "parallel", "arbitrary"), vmem_limit_bytes=VMEM_LIMIT),
        name="chunk_band",
    )(proj, proj, vt, bias_tab)


def _t5_bucket(rel):
    nb = T5_BUCKETS // 2
    max_exact = nb // 2
    n = jnp.abs(rel)
    large = max_exact + (jnp.log(jnp.maximum(n, 1).astype(jnp.float32) / max_exact)
                         / math.log(T5_MAX_DIST / max_exact) * (nb - max_exact)).astype(jnp.int32)
    large = jnp.minimum(large, nb - 1)
    return jnp.where(rel > 0, nb, 0) + jnp.where(n < max_exact, n, large)


def _toeplitz_tiles(per_distance):
    blk = ATT_BLOCK
    vec = per_distance.T.astype(F32)
    h, period = vec.shape
    skew = jnp.tile(vec, (1, blk))[:, :blk * (period - 1)].reshape(h, blk, period - 1)
    return jnp.stack([skew[:, :, blk - 1 + blk * d:2 * blk - 1 + blk * d] for d in range(3)], axis=1)


def _tile_distances():
    return jnp.arange(4 * ATT_BLOCK) - (ATT_BLOCK - 1)


def _tile_chunks():
    i = jnp.arange(ATT_BLOCK) // CHUNK
    return i[:, None], i[None, :]


def _t5_bias_tiles(t5_bias):
    assert ATT_BLOCK >= T5_MAX_DIST
    tiles = _toeplitz_tiles(t5_bias[_t5_bucket(-_tile_distances())])
    far = tiles[:, 2, :1, :1]
    tiles = (tiles[:, :2] - far[:, None]) * LOG2_E
    key_chunk, query_chunk = _tile_chunks()
    return tiles.at[:, 0].add(jnp.where(key_chunk <= query_chunk, 0.0, -jnp.inf))


def _band_bias_tiles(rel_bias):
    assert C_LEFT_CHUNKS * CHUNK == 2 * ATT_BLOCK
    tiles = _toeplitz_tiles(rel_bias[jnp.clip(_tile_distances(), -C_MAX_REL, C_MAX_REL) + C_MAX_REL]) * LOG2_E
    key_chunk, query_chunk = _tile_chunks()
    tiles = tiles.at[:, 0].add(jnp.where(key_chunk <= query_chunk, 0.0, -jnp.inf))
    return tiles.at[:, 2].add(jnp.where(key_chunk >= query_chunk, 0.0, -jnp.inf))


def _chunk_flags(*sections):
    flags = []
    for width, normed in sections:
        flags += [normed] * (width // LANES)
    return tuple(flags)


def kernel(x, t5_bias, norm_mix, norm_ffn, ab_w_in, a_q_norm, a_k_norm, a_lambda, a_subln, ab_w_out,
           c_w_in, c_q_norm, c_k_norm, c_rel_bias, c_w_out, ffn_w_up, ffn_conv_w, ffn_conv_b, ffn_w_down):
    b, seq, d = x.shape
    depth = norm_mix.shape[0]
    assert d == D_MODEL and seq % ROW_TILE == 0 and seq % ATT_BLOCK == 0
    scale = HEAD_DIM ** -0.5
    h = x.reshape(b * seq, d)
    t5_tiles = _t5_bias_tiles(t5_bias)
    ones = lambda n: jnp.ones((n,), F32)
    even_flags = _chunk_flags((2 * A_WIDTH, True), (2 * B_WIDTH, False))
    odd_flags = _chunk_flags((2 * C_WIDTH, True))
    for layer in range(depth):
        if layer % 2 == 0:
            e = layer // 2
            lam_init = 0.8 - 0.6 * math.exp(-0.3 * layer)
            w_in = ab_w_in[e].astype(BF16)
            gain = jnp.concatenate([
                jnp.tile(a_q_norm[e], 2 * A_HEADS) * (scale * LOG2_E), jnp.tile(a_k_norm[e], 2 * A_HEADS),
                ones(B_WIDTH) * scale, ones(B_WIDTH)])
            qk_cols = 2 * A_WIDTH
            w_main = jnp.concatenate([w_in[:, :qk_cols], w_in[:, 3 * A_WIDTH:3 * A_WIDTH + 2 * B_WIDTH]], axis=1)
            w_v = jnp.concatenate([w_in[:, qk_cols:3 * A_WIDTH], w_in[:, 3 * A_WIDTH + 2 * B_WIDTH:]], axis=1)
            proj, vt = _norm_proj(h, norm_mix[layer], w_main, gain, even_flags, w_v.T, seq)
            proj = proj.reshape(b, seq, -1)
            lam_p = a_lambda[e]
            lam = jnp.exp(jnp.sum(lam_p[0] * lam_p[1])) - jnp.exp(jnp.sum(lam_p[2] * lam_p[3])) + lam_init
            ya = _diff_attn(proj, vt, t5_tiles, a_subln[e], lam.astype(F32), 1.0 - lam_init)
            yb = _stick_attn(proj, vt)
            ys = [ya.reshape(b * seq, A_WIDTH), yb.reshape(b * seq, B_WIDTH)]
            w_out = ab_w_out[e]
        else:
            o = layer // 2
            w_in = c_w_in[o].astype(BF16)
            gain = jnp.concatenate([
                jnp.tile(c_q_norm[o], C_HEADS) * (scale * LOG2_E), jnp.tile(c_k_norm[o], C_HEADS)])
            proj, vt = _norm_proj(h, norm_mix[layer], w_in[:, :2 * C_WIDTH], gain, odd_flags,
                                  w_in[:, 2 * C_WIDTH:].T, seq)
            proj = proj.reshape(b, seq, -1)
            y = _band_attn(proj, vt, _band_bias_tiles(c_rel_bias[o]))
            ys = [y.reshape(b * seq, C_WIDTH)]
            w_out = c_w_out[o]
        h = _mix_ffn(h, ys, w_out.astype(BF16), norm_ffn[layer], ffn_w_up[layer].astype(BF16), ffn_conv_w[layer],
                     ffn_conv_b[layer], ffn_w_down[layer].astype(BF16), seq)
    return h.reshape(b, seq, d)
```

```python
import functools
import math

import jax
import jax.numpy as jnp
from jax import lax
from jax.experimental import pallas as pl
from jax.experimental.pallas import tpu as pltpu

D_MODEL = 1024
HEAD_DIM = 64
CHUNK = 64
A_HEADS = 4
B_HEADS = 8
C_HEADS = 16
A_WIDTH = A_HEADS * 2 * HEAD_DIM
B_WIDTH = B_HEADS * HEAD_DIM
C_WIDTH = C_HEADS * HEAD_DIM
T5_BUCKETS = 32
T5_MAX_DIST = 128
C_LEFT_CHUNKS = 8
C_MAX_REL = 128
D_FF = 2816
CONV_WIDTH = 3
EPS = 1e-6

LANES = 128
MXU_COLS = 256
ROW_TILE = 512
PROJ_ROW_TILE = 1024
ATT_BLOCK = 256
CONV_HALO = 16
FAR_GROUP = 6
DIFF_HEADS_PER_STEP = 4
BAND_PAIRS_PER_STEP = 4
STICK_PAIRS_PER_STEP = 4
ONES_ROWS = 16
VMEM_LIMIT = 56 * 1024 * 1024
LOG2_E = math.log2(math.e)

F32 = jnp.float32
BF16 = jnp.bfloat16


def _rms(x, g):
    ms = jnp.mean(x * x, axis=-1, keepdims=True)
    return x * lax.rsqrt(ms + EPS) * g


def _nt_dot(a, b):
    return lax.dot_general(a, b, (((1,), (1,)), ((), ())), preferred_element_type=F32)


def _lane_lo(shape):
    return lax.broadcasted_iota(jnp.int32, shape, len(shape) - 1) < HEAD_DIM


def _norm_proj_kernel(x_ref, g_ref, w_ref, gain_ref, wvt_ref, o_ref, vt_ref, *, norm_chunks):
    u = _rms(x_ref[...], g_ref[...]).astype(BF16)
    tm = u.shape[0]
    lo = _lane_lo((tm, LANES))
    n_cols = o_ref.shape[1]
    for c in range(n_cols // MXU_COLS):
        acc = jnp.dot(u, w_ref[:, c * MXU_COLS:(c + 1) * MXU_COLS], preferred_element_type=F32)
        for half in range(MXU_COLS // LANES):
            col = c * MXU_COLS + half * LANES
            y = acc[:, half * LANES:(half + 1) * LANES]
            if norm_chunks[col // LANES]:
                sq = y * y
                s0 = jnp.sum(jnp.where(lo, sq, 0.0), axis=-1, keepdims=True)
                s1 = jnp.sum(jnp.where(lo, 0.0, sq), axis=-1, keepdims=True)
                ms = jnp.where(lo, s0, s1) * (1.0 / HEAD_DIM)
                y = y * lax.rsqrt(ms + EPS)
            o_ref[:, col:col + LANES] = (y * gain_ref[:, col:col + LANES]).astype(BF16)
    vt = _nt_dot(wvt_ref[...], u).astype(BF16)
    for grp in range(vt_ref.shape[1]):
        for t in range(tm // ATT_BLOCK):
            vt_ref[0, grp, t, :LANES, :] = vt[grp * LANES:(grp + 1) * LANES, t * ATT_BLOCK:(t + 1) * ATT_BLOCK]
            vt_ref[0, grp, t, LANES:, :] = jnp.ones((ONES_ROWS, ATT_BLOCK), BF16)


def _norm_proj(h2, g, w, gain, norm_chunks, w_vt, seq):
    t, d = h2.shape
    n = w.shape[1]
    groups = w_vt.shape[0] // LANES
    tiles_per_seq = seq // PROJ_ROW_TILE
    tiles_per_step = PROJ_ROW_TILE // ATT_BLOCK
    v_rows = LANES + ONES_ROWS
    return pl.pallas_call(
        functools.partial(_norm_proj_kernel, norm_chunks=norm_chunks),
        grid=(t // PROJ_ROW_TILE,),
        in_specs=[
            pl.BlockSpec((PROJ_ROW_TILE, d), lambda i: (i, 0)),
            pl.BlockSpec((1, d), lambda i: (0, 0)),
            pl.BlockSpec((d, n), lambda i: (0, 0)),
            pl.BlockSpec((1, n), lambda i: (0, 0)),
            pl.BlockSpec(w_vt.shape, lambda i: (0, 0)),
        ],
        out_specs=[
            pl.BlockSpec((PROJ_ROW_TILE, n), lambda i: (i, 0)),
            pl.BlockSpec((1, groups, tiles_per_step, v_rows, ATT_BLOCK),
                         lambda i: (i // tiles_per_seq, 0, i % tiles_per_seq, 0, 0)),
        ],
        out_shape=[
            jax.ShapeDtypeStruct((t, n), BF16),
            jax.ShapeDtypeStruct((t // seq, groups, seq // ATT_BLOCK, v_rows, ATT_BLOCK), BF16),
        ],
        compiler_params=pltpu.CompilerParams(
            dimension_semantics=("parallel",), vmem_limit_bytes=VMEM_LIMIT),
        name="norm_proj",
    )(h2, g.reshape(1, d), w, gain.reshape(1, n), w_vt)


def _mix_ffn_kernel(*refs, n_y, tiles_per_seq):
    h_ref, halo_ref = refs[:2]
    y_refs, yhalo_refs = refs[2:2 + n_y], refs[2 + n_y:2 + 2 * n_y]
    wout_ref, g_ref, wup_ref, cw_ref, cb_ref, wdn_ref, o_ref, hd_ref = refs[2 + 2 * n_y:]

    def mixed(h, ys):
        row = 0
        for y_ref in ys:
            k = y_ref.shape[1]
            h = h + jnp.dot(y_ref[...], wout_ref[row:row + k, :], preferred_element_type=F32)
            row += k
        return h

    x = mixed(h_ref[...], y_refs)
    g = g_ref[...]
    tm = x.shape[0]
    keep_halo = jnp.where((pl.program_id(0) % tiles_per_seq) != 0, 1.0, 0.0)
    u_halo = _rms(mixed(halo_ref[...], yhalo_refs), g) * keep_halo
    u_ext = jnp.concatenate([u_halo, _rms(x, g)], axis=0).astype(BF16)

    def conv(col):
        hd_ref[...] = jnp.dot(u_ext, wup_ref[:, col:col + D_FF], preferred_element_type=F32)
        out = cb_ref[:, col:col + D_FF]
        for back in range(CONV_WIDTH):
            tap = cw_ref[CONV_WIDTH - 1 - back:CONV_WIDTH - back, col:col + D_FF]
            out = out + hd_ref[pl.ds(CONV_HALO - back, tm), :] * tap
        return out

    gate = conv(0)
    val = conv(D_FF)
    act = (gate * jax.nn.sigmoid(gate) * val).astype(BF16)
    o_ref[...] = x + jnp.dot(act, wdn_ref[...], preferred_element_type=F32)


def _mix_ffn(h2, ys, w_out, g, w_up, conv_w, conv_b, w_down, seq):
    t, d = h2.shape
    halo_blocks = ROW_TILE // CONV_HALO
    const = lambda i: (0, 0)
    tile = lambda i: (i, 0)
    halo = lambda i: (jnp.maximum(i * halo_blocks - 1, 0), 0)
    resident = lambda a: pl.BlockSpec(a.shape, const, pipeline_mode=pl.Buffered(1))
    in_specs = [pl.BlockSpec((ROW_TILE, d), tile), pl.BlockSpec((CONV_HALO, d), halo)]
    in_specs += [pl.BlockSpec((ROW_TILE, y.shape[1]), tile) for y in ys]
    in_specs += [pl.BlockSpec((CONV_HALO, y.shape[1]), halo) for y in ys]
    in_specs += [resident(w_out), pl.BlockSpec((1, d), const), resident(w_up), pl.BlockSpec(conv_w.shape, const),
                 pl.BlockSpec((1, 2 * D_FF), const), resident(w_down)]
    return pl.pallas_call(
        functools.partial(_mix_ffn_kernel, n_y=len(ys), tiles_per_seq=seq // ROW_TILE),
        grid=(t // ROW_TILE,),
        in_specs=in_specs,
        out_specs=pl.BlockSpec((ROW_TILE, d), tile),
        out_shape=jax.ShapeDtypeStruct((t, d), F32),
        scratch_shapes=[pltpu.VMEM((ROW_TILE + CONV_HALO, D_FF), F32)],
        compiler_params=pltpu.CompilerParams(
            dimension_semantics=("parallel",), vmem_limit_bytes=VMEM_LIMIT),
        name="mix_out_ffn",
    )(h2, h2, *ys, *ys, w_out, g.reshape(1, d), w_up, conv_w, conv_b.reshape(1, 2 * D_FF), w_down)


def _split_heads(q):
    lo = _lane_lo(q.shape)
    zero = jnp.zeros_like(q)
    return jnp.concatenate([jnp.where(lo, q, zero), jnp.where(lo, zero, q)], axis=0)


def _tile_ids(shape):
    return (lax.broadcasted_iota(jnp.int32, shape, 0), lax.broadcasted_iota(jnp.int32, shape, 1))


def _diff_attn_kernel(lam_ref, q_ref, k_ref, vt_ref, bias_ref, subln_ref, o_ref,
                      s_ref, cmax_ref, m_ref, acc_ref, *, out_scale):
    blk = ATT_BLOCK
    heads = range(DIFF_HEADS_PER_STEP)
    qi = pl.program_id(2)
    lanes = lambda h: slice(h * LANES, (h + 1) * LANES)
    qq = [_split_heads(q_ref[0, :, lanes(h)]) for h in heads]
    m_ref[...] = jnp.full(m_ref.shape, -jnp.inf, F32)
    acc_ref[...] = jnp.zeros(acc_ref.shape, F32)
    n_far = jnp.maximum(qi - 1, 0)
    both = lambda t: jnp.concatenate([t, t], axis=1)

    def scores_to(slot, kj, finish):
        start = pl.multiple_of(kj * blk, blk)
        for h in heads:
            s = finish(h, _nt_dot(k_ref[0, pl.ds(start, blk), lanes(h)], qq[h]))
            s_ref[h, slot] = s
            cmax_ref[h, slot] = jnp.max(s_ref[h, slot], axis=0, keepdims=True)

    def absorb(slot, kj):
        for h in heads:
            m_prev = m_ref[h]
            m_new = jnp.maximum(m_prev, cmax_ref[h, slot])
            alpha = jnp.exp2(m_prev - m_new)
            p = jnp.exp2(s_ref[h, slot] - m_new).astype(BF16)
            pv = jnp.dot(vt_ref[0, h, kj], p, preferred_element_type=F32)
            acc_ref[h] = alpha * acc_ref[h] + pv
            m_ref[h] = m_new

    def far_key_tile(i):
        return jnp.maximum(qi - 2 - i, 0)

    prev = jnp.maximum(qi - 1, 0)
    no_prev = jnp.where(qi >= 1, 0.0, -jnp.inf)
    scores_to(0, qi, lambda h, s: s + both(bias_ref[h, 0]))
    scores_to(1, prev, lambda h, s: s + both(bias_ref[h, 1] + no_prev))
    absorb(0, qi)
    scores_to(0, far_key_tile(0), lambda h, s: s)
    absorb(1, prev)

    def far_steps(first, steps, finish_for):
        for u in range(steps):
            scores_to((u + 1) % 2, far_key_tile(first + u + 1), finish_for(first + u + 1))
            absorb(u % 2, far_key_tile(first + u))

    full_trips = n_far // FAR_GROUP

    def full_trip(g, carry):
        far_steps(g * FAR_GROUP, FAR_GROUP, lambda i: (lambda h, s: s))
        return carry

    lax.fori_loop(0, full_trips, full_trip, 0)

    done = full_trips * FAR_GROUP

    def tail_trip(t, carry):
        far_steps(done + 2 * t, 2, lambda i: (lambda h, s: s + jnp.where(i < n_far, 0.0, -jnp.inf)))
        return carry

    lax.fori_loop(0, (n_far - done + 1) // 2, tail_trip, 0)

    for h in heads:
        acc = acc_ref[h]
        o = acc[:LANES] / acc[LANES:LANES + 1]
        ya = o[:, :blk] - lam_ref[0, 0] * o[:, blk:]
        ms = jnp.mean(ya * ya, axis=0, keepdims=True)
        ya = ya * lax.rsqrt(ms + EPS) * (subln_ref[...] * out_scale)
        o_ref[0, :, lanes(h)] = ya.T.astype(BF16)


def _diff_attn(proj, vt, bias_tab, subln, lam, out_scale):
    b, s, _ = proj.shape
    nq = s // ATT_BLOCK
    hps = DIFF_HEADS_PER_STEP
    width = hps * LANES
    q_blk, k_blk = 0, A_WIDTH // width
    v_rows = LANES + ONES_ROWS
    return pl.pallas_call(
        functools.partial(_diff_attn_kernel, out_scale=out_scale),
        grid=(b, A_HEADS // hps, nq),
        in_specs=[
            pl.BlockSpec(memory_space=pltpu.SMEM),
            pl.BlockSpec((1, ATT_BLOCK, width), lambda bi, g, qi: (bi, qi, q_blk + g)),
            pl.BlockSpec((1, s, width), lambda bi, g, qi: (bi, 0, k_blk + g)),
            pl.BlockSpec((1, hps, nq, v_rows, ATT_BLOCK), lambda bi, g, qi: (bi, g, 0, 0, 0)),
            pl.BlockSpec((hps, 2, ATT_BLOCK, ATT_BLOCK), lambda bi, g, qi: (g, 0, 0, 0)),
            pl.BlockSpec((LANES, 1), lambda bi, g, qi: (0, 0)),
        ],
        out_specs=pl.BlockSpec((1, ATT_BLOCK, width), lambda bi, g, qi: (bi, qi, g)),
        out_shape=jax.ShapeDtypeStruct((b, s, A_WIDTH), BF16),
        scratch_shapes=[
            pltpu.VMEM((hps, 2, ATT_BLOCK, 2 * ATT_BLOCK), F32),
            pltpu.VMEM((hps, 2, 1, 2 * ATT_BLOCK), F32),
            pltpu.VMEM((hps, 1, 2 * ATT_BLOCK), F32),
            pltpu.VMEM((hps, v_rows, 2 * ATT_BLOCK), F32),
        ],
        compiler_params=pltpu.CompilerParams(
            dimension_semantics=("parallel", "parallel", "arbitrary"), vmem_limit_bytes=VMEM_LIMIT),
        name="diff_attn",
    )(lam.reshape(1, 1), proj, proj, vt, bias_tab, subln.reshape(LANES, 1))


STICK_DEAD_LOGIT = -104.0


def _stick_kernel(q_ref, k_ref, vt_ref, o_ref, carry_ref, acc_ref):
    blk = ATT_BLOCK
    pairs = range(STICK_PAIRS_PER_STEP)
    qi = pl.program_id(2)
    lanes = lambda g: slice(g * LANES, (g + 1) * LANES)
    qq = [_split_heads(q_ref[0, :, lanes(g)]) for g in pairs]
    carry_ref[...] = jnp.zeros(carry_ref.shape, F32)
    acc_ref[...] = jnp.zeros(acc_ref.shape, F32)
    kk, qc = _tile_ids((blk, blk))
    later = jnp.where(qc > kk, 1.0, 0.0).astype(BF16)
    causal = jnp.concatenate([kk < qc, kk < qc], axis=1)

    def update(kj, diagonal):
        start = pl.multiple_of(kj * blk, blk)
        logits = []
        for g in pairs:
            z = _nt_dot(k_ref[0, pl.ds(start, blk), lanes(g)], qq[g])
            log_1m = -(jnp.maximum(z, 0.0) + jnp.log(1.0 + jnp.exp(-jnp.abs(z))))
            if diagonal:
                log_1m = jnp.where(causal, log_1m, 0.0)
            hi = log_1m.astype(BF16)
            lo = (log_1m - hi.astype(F32)).astype(BF16)
            remain = (jnp.dot(later, hi, preferred_element_type=F32)
                      + jnp.dot(later, lo, preferred_element_type=F32))
            carry = carry_ref[g]
            logits.append(z + log_1m + remain + carry)
            carry_ref[g] = carry + (remain[:1] + log_1m[:1])
        for g in pairs:
            a = jnp.exp(logits[g])
            if diagonal:
                a = jnp.where(causal, a, 0.0)
            acc_ref[g] += jnp.dot(vt_ref[0, g, kj, :LANES, :], a.astype(BF16), preferred_element_type=F32)

    def live():
        return jnp.max(carry_ref[...]) > STICK_DEAD_LOGIT

    update(qi, True)

    def body(state):
        j, _ = state
        update(qi - 1 - j, False)
        return j + 1, live()

    lax.while_loop(lambda st: jnp.logical_and(st[0] < qi, st[1]), body, (jnp.int32(0), live()))

    for g in pairs:
        acc = acc_ref[g]
        o = jnp.concatenate([acc[:HEAD_DIM, :blk], acc[HEAD_DIM:, blk:]], axis=0)
        o_ref[0, :, lanes(g)] = o.T.astype(BF16)


def _stick_attn(proj, vt):
    b, s, _ = proj.shape
    nq = s // ATT_BLOCK
    pps = STICK_PAIRS_PER_STEP
    width = pps * LANES
    base = 2 * A_WIDTH // width
    q_blk, k_blk = base, base + B_WIDTH // width
    vt_blk = A_HEADS // pps
    return pl.pallas_call(
        _stick_kernel,
        grid=(b, B_WIDTH // width, nq),
        in_specs=[
            pl.BlockSpec((1, ATT_BLOCK, width), lambda bi, g, qi: (bi, qi, q_blk + g)),
            pl.BlockSpec((1, s, width), lambda bi, g, qi: (bi, 0, k_blk + g)),
            pl.BlockSpec((1, pps, nq, LANES + ONES_ROWS, ATT_BLOCK), lambda bi, g, qi: (bi, vt_blk + g, 0, 0, 0)),
        ],
        out_specs=pl.BlockSpec((1, ATT_BLOCK, width), lambda bi, g, qi: (bi, qi, g)),
        out_shape=jax.ShapeDtypeStruct((b, s, B_WIDTH), BF16),
        scratch_shapes=[
            pltpu.VMEM((pps, 1, 2 * ATT_BLOCK), F32),
            pltpu.VMEM((pps, LANES, 2 * ATT_BLOCK), F32),
        ],
        compiler_params=pltpu.CompilerParams(
            dimension_semantics=("parallel", "parallel", "arbitrary"), vmem_limit_bytes=VMEM_LIMIT),
        name="stick_breaking",
    )(proj, proj, vt)


def _band_kernel(q_ref, k_ref, vt_ref, bias_ref, o_ref, s_ref):
    blk = ATT_BLOCK
    qi = pl.program_id(2)
    key_tiles = [jnp.maximum(qi - d, 0) for d in range(3)]
    padding = [None] + [jnp.where(qi >= d, 0.0, -jnp.inf) for d in (1, 2)]

    def score(g):
        lanes = slice(g * LANES, (g + 1) * LANES)
        qq = _split_heads(q_ref[0, :, lanes])
        tile_max = []
        for d in range(3):
            start = pl.multiple_of(key_tiles[d] * blk, blk)
            s = _nt_dot(k_ref[0, pl.ds(start, blk), lanes], qq)
            bias = jnp.concatenate([bias_ref[2 * g, d], bias_ref[2 * g + 1, d]], axis=1)
            s = s + (bias if padding[d] is None else bias + padding[d])
            s_ref[g, d] = s
            tile_max.append(jnp.max(s_ref[g, d], axis=0, keepdims=True))
        return functools.reduce(jnp.maximum, tile_max)

    def attend(g, m):
        lanes = slice(g * LANES, (g + 1) * LANES)
        acc = jnp.zeros((LANES + ONES_ROWS, 2 * blk), F32)
        for d in range(3):
            p = jnp.exp2(s_ref[g, d] - m).astype(BF16)
            acc = acc + jnp.dot(vt_ref[0, g, key_tiles[d]], p, preferred_element_type=F32)
        o = acc[:LANES] / acc[LANES:LANES + 1]
        o = jnp.concatenate([o[:HEAD_DIM, :blk], o[HEAD_DIM:, blk:]], axis=0)
        o_ref[0, :, lanes] = o.T.astype(BF16)

    pending = score(0)
    for g in range(BAND_PAIRS_PER_STEP):
        following = score(g + 1) if g + 1 < BAND_PAIRS_PER_STEP else None
        attend(g, pending)
        pending = following


def _band_attn(proj, vt, bias_tab):
    b, s, _ = proj.shape
    nq = s // ATT_BLOCK
    pps = BAND_PAIRS_PER_STEP
    width = pps * LANES
    q_blk, k_blk = 0, C_WIDTH // width
    return pl.pallas_call(
        _band_kernel,
        grid=(b, C_WIDTH // width, nq),
        in_specs=[
            pl.BlockSpec((1, ATT_BLOCK, width), lambda bi, g, qi: (bi, qi, q_blk + g)),
            pl.BlockSpec((1, s, width), lambda bi, g, qi: (bi, 0, k_blk + g)),
            pl.BlockSpec((1, pps, nq, LANES + ONES_ROWS, ATT_BLOCK), lambda bi, g, qi: (bi, g, 0, 0, 0)),
            pl.BlockSpec((2 * pps, 3, ATT_BLOCK, ATT_BLOCK), lambda bi, g, qi: (g, 0, 0, 0)),
        ],
        out_specs=pl.BlockSpec((1, ATT_BLOCK, width), lambda bi, g, qi: (bi, qi, g)),
        out_shape=jax.ShapeDtypeStruct((b, s, C_WIDTH), BF16),
        scratch_shapes=[pltpu.VMEM((pps, 3, ATT_BLOCK, 2 * ATT_BLOCK), F32)],
        compiler_params=pltpu.CompilerParams(
            dimension_semantics=("parallel", "parallel", "arbitrary"), vmem_limit_bytes=VMEM_LIMIT),
        name="chunk_band",
    )(proj, proj, vt, bias_tab)


def _t5_bucket(rel):
    nb = T5_BUCKETS // 2
    max_exact = nb // 2
    n = jnp.abs(rel)
    large = max_exact + (jnp.log(jnp.maximum(n, 1).astype(jnp.float32) / max_exact)
                         / math.log(T5_MAX_DIST / max_exact) * (nb - max_exact)).astype(jnp.int32)
    large = jnp.minimum(large, nb - 1)
    return jnp.where(rel > 0, nb, 0) + jnp.where(n < max_exact, n, large)


def _toeplitz_tiles(per_distance):
    blk = ATT_BLOCK
    vec = per_distance.T.astype(F32)
    h, period = vec.shape
    skew = jnp.tile(vec, (1, blk))[:, :blk * (period - 1)].reshape(h, blk, period - 1)
    return jnp.stack([skew[:, :, blk - 1 + blk * d:2 * blk - 1 + blk * d] for d in range(3)], axis=1)


def _tile_distances():
    return jnp.arange(4 * ATT_BLOCK) - (ATT_BLOCK - 1)


def _tile_chunks():
    i = jnp.arange(ATT_BLOCK) // CHUNK
    return i[:, None], i[None, :]


def _t5_bias_tiles(t5_bias):
    assert ATT_BLOCK >= T5_MAX_DIST
    tiles = _toeplitz_tiles(t5_bias[_t5_bucket(-_tile_distances())])
    far = tiles[:, 2, :1, :1]
    tiles = (tiles[:, :2] - far[:, None]) * LOG2_E
    key_chunk, query_chunk = _tile_chunks()
    return tiles.at[:, 0].add(jnp.where(key_chunk <= query_chunk, 0.0, -jnp.inf))


def _band_bias_tiles(rel_bias):
    assert C_LEFT_CHUNKS * CHUNK == 2 * ATT_BLOCK
    tiles = _toeplitz_tiles(rel_bias[jnp.clip(_tile_distances(), -C_MAX_REL, C_MAX_REL) + C_MAX_REL]) * LOG2_E
    key_chunk, query_chunk = _tile_chunks()
    tiles = tiles.at[:, 0].add(jnp.where(key_chunk <= query_chunk, 0.0, -jnp.inf))
    return tiles.at[:, 2].add(jnp.where(key_chunk >= query_chunk, 0.0, -jnp.inf))


def _chunk_flags(*sections):
    flags = []
    for width, normed in sections:
        flags += [normed] * (width // LANES)
    return tuple(flags)


def kernel(x, t5_bias, norm_mix, norm_ffn, ab_w_in, a_q_norm, a_k_norm, a_lambda, a_subln, ab_w_out,
           c_w_in, c_q_norm, c_k_norm, c_rel_bias, c_w_out, ffn_w_up, ffn_conv_w, ffn_conv_b, ffn_w_down):
    b, seq, d = x.shape
    depth = norm_mix.shape[0]
    assert d == D_MODEL and seq % ROW_TILE == 0 and seq % PROJ_ROW_TILE == 0 and seq % ATT_BLOCK == 0
    scale = HEAD_DIM ** -0.5
    h = x.reshape(b * seq, d)
    t5_tiles = _t5_bias_tiles(t5_bias)
    ones = lambda n: jnp.ones((n,), F32)
    even_flags = _chunk_flags((2 * A_WIDTH, True), (2 * B_WIDTH, False))
    odd_flags = _chunk_flags((2 * C_WIDTH, True))
    for layer in range(depth):
        if layer % 2 == 0:
            e = layer // 2
            lam_init = 0.8 - 0.6 * math.exp(-0.3 * layer)
            w_in = ab_w_in[e].astype(BF16)
            gain = jnp.concatenate([
                jnp.tile(a_q_norm[e], 2 * A_HEADS) * (scale * LOG2_E), jnp.tile(a_k_norm[e], 2 * A_HEADS),
                ones(B_WIDTH) * scale, ones(B_WIDTH)])
            qk_cols = 2 * A_WIDTH
            w_main = jnp.concatenate([w_in[:, :qk_cols], w_in[:, 3 * A_WIDTH:3 * A_WIDTH + 2 * B_WIDTH]], axis=1)
            w_v = jnp.concatenate([w_in[:, qk_cols:3 * A_WIDTH], w_in[:, 3 * A_WIDTH + 2 * B_WIDTH:]], axis=1)
            proj, vt = _norm_proj(h, norm_mix[layer], w_main, gain, even_flags, w_v.T, seq)
            proj = proj.reshape(b, seq, -1)
            lam_p = a_lambda[e]
            lam = jnp.exp(jnp.sum(lam_p[0] * lam_p[1])) - jnp.exp(jnp.sum(lam_p[2] * lam_p[3])) + lam_init
            ya = _diff_attn(proj, vt, t5_tiles, a_subln[e], lam.astype(F32), 1.0 - lam_init)
            yb = _stick_attn(proj, vt)
            ys = [ya.reshape(b * seq, A_WIDTH), yb.reshape(b * seq, B_WIDTH)]
            w_out = ab_w_out[e]
        else:
            o = layer // 2
            w_in = c_w_in[o].astype(BF16)
            gain = jnp.concatenate([
                jnp.tile(c_q_norm[o], C_HEADS) * (scale * LOG2_E), jnp.tile(c_k_norm[o], C_HEADS)])
            proj, vt = _norm_proj(h, norm_mix[layer], w_in[:, :2 * C_WIDTH], gain, odd_flags,
                                  w_in[:, 2 * C_WIDTH:].T, seq)
            proj = proj.reshape(b, seq, -1)
            y = _band_attn(proj, vt, _band_bias_tiles(c_rel_bias[o]))
            ys = [y.reshape(b * seq, C_WIDTH)]
            w_out = c_w_out[o]
        h = _mix_ffn(h, ys, w_out.astype(BF16), norm_ffn[layer], ffn_w_up[layer].astype(BF16), ffn_conv_w[layer],
                     ffn_conv_b[layer], ffn_w_down[layer].astype(BF16), seq)
    return h.reshape(b, seq, d)
```

```python
import functools
import math

import jax
import jax.numpy as jnp
from jax import lax
from jax.experimental import pallas as pl
from jax.experimental.pallas import tpu as pltpu

D_MODEL = 1024
HEAD_DIM = 64
CHUNK = 64
A_HEADS = 4
B_HEADS = 8
C_HEADS = 16
A_WIDTH = A_HEADS * 2 * HEAD_DIM
B_WIDTH = B_HEADS * HEAD_DIM
C_WIDTH = C_HEADS * HEAD_DIM
T5_BUCKETS = 32
T5_MAX_DIST = 128
C_LEFT_CHUNKS = 8
C_MAX_REL = 128
D_FF = 2816
CONV_WIDTH = 3
EPS = 1e-6

LANES = 128
MXU_COLS = 256
ROW_TILE = 512
PROJ_ROW_TILE = 1024
ATT_BLOCK = 256
CONV_HALO = 16
FAR_GROUP = 6
DIFF_HEADS_PER_STEP = 4
BAND_PAIRS_PER_STEP = 4
STICK_PAIRS_PER_STEP = 4
ONES_ROWS = 16
VMEM_LIMIT = 56 * 1024 * 1024
LOG2_E = math.log2(math.e)

F32 = jnp.float32
BF16 = jnp.bfloat16


def _rms(x, g):
    ms = jnp.mean(x * x, axis=-1, keepdims=True)
    return x * lax.rsqrt(ms + EPS) * g


def _nt_dot(a, b):
    return lax.dot_general(a, b, (((1,), (1,)), ((), ())), preferred_element_type=F32)


def _lane_lo(shape):
    return lax.broadcasted_iota(jnp.int32, shape, len(shape) - 1) < HEAD_DIM


def _norm_proj_kernel(x_ref, g_ref, w_ref, gain_ref, wvt_ref, o_ref, vt_ref, *, norm_chunks):
    u = _rms(x_ref[...], g_ref[...]).astype(BF16)
    tm = u.shape[0]
    lo = _lane_lo((tm, LANES))
    n_cols = o_ref.shape[1]
    for c in range(n_cols // MXU_COLS):
        acc = jnp.dot(u, w_ref[:, c * MXU_COLS:(c + 1) * MXU_COLS], preferred_element_type=F32)
        for half in range(MXU_COLS // LANES):
            col = c * MXU_COLS + half * LANES
            y = acc[:, half * LANES:(half + 1) * LANES]
            if norm_chunks[col // LANES]:
                sq = y * y
                s0 = jnp.sum(jnp.where(lo, sq, 0.0), axis=-1, keepdims=True)
                s1 = jnp.sum(jnp.where(lo, 0.0, sq), axis=-1, keepdims=True)
                ms = jnp.where(lo, s0, s1) * (1.0 / HEAD_DIM)
                y = y * lax.rsqrt(ms + EPS)
            o_ref[:, col:col + LANES] = (y * gain_ref[:, col:col + LANES]).astype(BF16)
    vt = _nt_dot(wvt_ref[...], u).astype(BF16)
    for grp in range(vt_ref.shape[1]):
        for t in range(tm // ATT_BLOCK):
            vt_ref[0, grp, t, :LANES, :] = vt[grp * LANES:(grp + 1) * LANES, t * ATT_BLOCK:(t + 1) * ATT_BLOCK]
            vt_ref[0, grp, t, LANES:, :] = jnp.ones((ONES_ROWS, ATT_BLOCK), BF16)


def _norm_proj(h2, g, w, gain, norm_chunks, w_vt, seq):
    t, d = h2.shape
    n = w.shape[1]
    groups = w_vt.shape[0] // LANES
    tiles_per_seq = seq // PROJ_ROW_TILE
    tiles_per_step = PROJ_ROW_TILE // ATT_BLOCK
    v_rows = LANES + ONES_ROWS
    return pl.pallas_call(
        functools.partial(_norm_proj_kernel, norm_chunks=norm_chunks),
        grid=(t // PROJ_ROW_TILE,),
        in_specs=[
            pl.BlockSpec((PROJ_ROW_TILE, d), lambda i: (i, 0)),
            pl.BlockSpec((1, d), lambda i: (0, 0)),
            pl.BlockSpec((d, n), lambda i: (0, 0)),
            pl.BlockSpec((1, n), lambda i: (0, 0)),
            pl.BlockSpec(w_vt.shape, lambda i: (0, 0)),
        ],
        out_specs=[
            pl.BlockSpec((PROJ_ROW_TILE, n), lambda i: (i, 0)),
            pl.BlockSpec((1, groups, tiles_per_step, v_rows, ATT_BLOCK),
                         lambda i: (i // tiles_per_seq, 0, i % tiles_per_seq, 0, 0)),
        ],
        out_shape=[
            jax.ShapeDtypeStruct((t, n), BF16),
            jax.ShapeDtypeStruct((t // seq, groups, seq // ATT_BLOCK, v_rows, ATT_BLOCK), BF16),
        ],
        compiler_params=pltpu.CompilerParams(
            dimension_semantics=("parallel",), vmem_limit_bytes=VMEM_LIMIT),
        name="norm_proj",
    )(h2, g.reshape(1, d), w, gain.reshape(1, n), w_vt)


def _mix_ffn_kernel(*refs, n_y, tiles_per_seq):
    h_ref, halo_ref = refs[:2]
    y_refs, yhalo_refs = refs[2:2 + n_y], refs[2 + n_y:2 + 2 * n_y]
    wout_ref, g_ref, wup_ref, cw_ref, cb_ref, wdn_ref, o_ref, hd_ref = refs[2 + 2 * n_y:]

    def mixed(h, ys):
        row = 0
        for y_ref in ys:
            k = y_ref.shape[1]
            h = h + jnp.dot(y_ref[...], wout_ref[row:row + k, :], preferred_element_type=F32)
            row += k
        return h

    x = mixed(h_ref[...], y_refs)
    g = g_ref[...]
    tm = x.shape[0]
    keep_halo = jnp.where((pl.program_id(0) % tiles_per_seq) != 0, 1.0, 0.0)
    u_halo = _rms(mixed(halo_ref[...], yhalo_refs), g) * keep_halo
    u_ext = jnp.concatenate([u_halo, _rms(x, g)], axis=0).astype(BF16)

    def conv(col):
        hd_ref[...] = jnp.dot(u_ext, wup_ref[:, col:col + D_FF], preferred_element_type=F32)
        out = cb_ref[:, col:col + D_FF]
        for back in range(CONV_WIDTH):
            tap = cw_ref[CONV_WIDTH - 1 - back:CONV_WIDTH - back, col:col + D_FF]
            out = out + hd_ref[pl.ds(CONV_HALO - back, tm), :] * tap
        return out

    gate = conv(0)
    val = conv(D_FF)
    act = (gate * jax.nn.sigmoid(gate) * val).astype(BF16)
    o_ref[...] = x + jnp.dot(act, wdn_ref[...], preferred_element_type=F32)


def _mix_ffn(h2, ys, w_out, g, w_up, conv_w, conv_b, w_down, seq):
    t, d = h2.shape
    halo_blocks = ROW_TILE // CONV_HALO
    const = lambda i: (0, 0)
    tile = lambda i: (i, 0)
    halo = lambda i: (jnp.maximum(i * halo_blocks - 1, 0), 0)
    resident = lambda a: pl.BlockSpec(a.shape, const, pipeline_mode=pl.Buffered(1))
    in_specs = [pl.BlockSpec((ROW_TILE, d), tile), pl.BlockSpec((CONV_HALO, d), halo)]
    in_specs += [pl.BlockSpec((ROW_TILE, y.shape[1]), tile) for y in ys]
    in_specs += [pl.BlockSpec((CONV_HALO, y.shape[1]), halo) for y in ys]
    in_specs += [resident(w_out), pl.BlockSpec((1, d), const), resident(w_up), pl.BlockSpec(conv_w.shape, const),
                 pl.BlockSpec((1, 2 * D_FF), const), resident(w_down)]
    return pl.pallas_call(
        functools.partial(_mix_ffn_kernel, n_y=len(ys), tiles_per_seq=seq // ROW_TILE),
        grid=(t // ROW_TILE,),
        in_specs=in_specs,
        out_specs=pl.BlockSpec((ROW_TILE, d), tile),
        out_shape=jax.ShapeDtypeStruct((t, d), F32),
        scratch_shapes=[pltpu.VMEM((ROW_TILE + CONV_HALO, D_FF), F32)],
        compiler_params=pltpu.CompilerParams(
            dimension_semantics=("parallel",), vmem_limit_bytes=VMEM_LIMIT),
        name="mix_out_ffn",
    )(h2, h2, *ys, *ys, w_out, g.reshape(1, d), w_up, conv_w, conv_b.reshape(1, 2 * D_FF), w_down)


def _split_heads(q):
    lo = _lane_lo(q.shape)
    zero = jnp.zeros_like(q)
    return jnp.concatenate([jnp.where(lo, q, zero), jnp.where(lo, zero, q)], axis=0)


def _tile_ids(shape):
    return (lax.broadcasted_iota(jnp.int32, shape, 0), lax.broadcasted_iota(jnp.int32, shape, 1))


def _diff_attn_kernel(lam_ref, q_ref, k_ref, vt_ref, bias_ref, subln_ref, o_ref,
                      s_ref, cmax_ref, m_ref, acc_ref, *, out_scale):
    blk = ATT_BLOCK
    heads = range(DIFF_HEADS_PER_STEP)
    qi = pl.program_id(2)
    lanes = lambda h: slice(h * LANES, (h + 1) * LANES)
    qq = [_split_heads(q_ref[0, :, lanes(h)]) for h in heads]
    m_ref[...] = jnp.full(m_ref.shape, -jnp.inf, F32)
    acc_ref[...] = jnp.zeros(acc_ref.shape, F32)
    n_far = jnp.maximum(qi - 1, 0)
    both = lambda t: jnp.concatenate([t, t], axis=1)

    def scores_to(slot, kj, finish):
        start = pl.multiple_of(kj * blk, blk)
        for h in heads:
            s = finish(h, _nt_dot(k_ref[0, pl.ds(start, blk), lanes(h)], qq[h]))
            s_ref[h, slot] = s
            cmax_ref[h, slot] = jnp.max(s_ref[h, slot], axis=0, keepdims=True)

    def absorb(slot, kj):
        for h in heads:
            m_prev = m_ref[h]
            m_new = jnp.maximum(m_prev, cmax_ref[h, slot])
            alpha = jnp.exp2(m_prev - m_new)
            p = jnp.exp2(s_ref[h, slot] - m_new).astype(BF16)
            pv = jnp.dot(vt_ref[0, h, kj], p, preferred_element_type=F32)
            acc_ref[h] = alpha * acc_ref[h] + pv
            m_ref[h] = m_new

    def far_key_tile(i):
        return jnp.maximum(qi - 2 - i, 0)

    prev = jnp.maximum(qi - 1, 0)
    no_prev = jnp.where(qi >= 1, 0.0, -jnp.inf)
    scores_to(0, qi, lambda h, s: s + both(bias_ref[h, 0]))
    scores_to(1, prev, lambda h, s: s + both(bias_ref[h, 1] + no_prev))
    absorb(0, qi)
    scores_to(0, far_key_tile(0), lambda h, s: s)
    absorb(1, prev)

    def far_steps(first, steps, finish_for):
        for u in range(steps):
            scores_to((u + 1) % 2, far_key_tile(first + u + 1), finish_for(first + u + 1))
            absorb(u % 2, far_key_tile(first + u))

    full_trips = n_far // FAR_GROUP

    def full_trip(g, carry):
        far_steps(g * FAR_GROUP, FAR_GROUP, lambda i: (lambda h, s: s))
        return carry

    lax.fori_loop(0, full_trips, full_trip, 0)

    done = full_trips * FAR_GROUP

    def tail_trip(t, carry):
        far_steps(done + 2 * t, 2, lambda i: (lambda h, s: s + jnp.where(i < n_far, 0.0, -jnp.inf)))
        return carry

    lax.fori_loop(0, (n_far - done + 1) // 2, tail_trip, 0)

    for h in heads:
        acc = acc_ref[h]
        o = acc[:LANES] / acc[LANES:LANES + 1]
        ya = o[:, :blk] - lam_ref[0, 0] * o[:, blk:]
        ms = jnp.mean(ya * ya, axis=0, keepdims=True)
        ya = ya * lax.rsqrt(ms + EPS) * (subln_ref[...] * out_scale)
        o_ref[0, :, lanes(h)] = ya.T.astype(BF16)


def _diff_attn(proj, vt, bias_tab, subln, lam, out_scale):
    b, s, _ = proj.shape
    nq = s // ATT_BLOCK
    hps = DIFF_HEADS_PER_STEP
    width = hps * LANES
    q_blk, k_blk = 0, A_WIDTH // width
    v_rows = LANES + ONES_ROWS
    return pl.pallas_call(
        functools.partial(_diff_attn_kernel, out_scale=out_scale),
        grid=(b, A_HEADS // hps, nq),
        in_specs=[
            pl.BlockSpec(memory_space=pltpu.SMEM),
            pl.BlockSpec((1, ATT_BLOCK, width), lambda bi, g, qi: (bi, qi, q_blk + g)),
            pl.BlockSpec((1, s, width), lambda bi, g, qi: (bi, 0, k_blk + g)),
            pl.BlockSpec((1, hps, nq, v_rows, ATT_BLOCK), lambda bi, g, qi: (bi, g, 0, 0, 0)),
            pl.BlockSpec((hps, 2, ATT_BLOCK, ATT_BLOCK), lambda bi, g, qi: (g, 0, 0, 0)),
            pl.BlockSpec((LANES, 1), lambda bi, g, qi: (0, 0)),
        ],
        out_specs=pl.BlockSpec((1, ATT_BLOCK, width), lambda bi, g, qi: (bi, qi, g)),
        out_shape=jax.ShapeDtypeStruct((b, s, A_WIDTH), BF16),
        scratch_shapes=[
            pltpu.VMEM((hps, 2, ATT_BLOCK, 2 * ATT_BLOCK), F32),
            pltpu.VMEM((hps, 2, 1, 2 * ATT_BLOCK), F32),
            pltpu.VMEM((hps, 1, 2 * ATT_BLOCK), F32),
            pltpu.VMEM((hps, v_rows, 2 * ATT_BLOCK), F32),
        ],
        compiler_params=pltpu.CompilerParams(
            dimension_semantics=("parallel", "parallel", "arbitrary"), vmem_limit_bytes=VMEM_LIMIT),
        name="diff_attn",
    )(lam.reshape(1, 1), proj, proj, vt, bias_tab, subln.reshape(LANES, 1))


STICK_DEAD_LOGIT = -104.0


def _stick_kernel(q_ref, k_ref, vt_ref, o_ref, carry_ref, acc_ref):
    blk = ATT_BLOCK
    pairs = range(STICK_PAIRS_PER_STEP)
    qi = pl.program_id(2)
    lanes = lambda g: slice(g * LANES, (g + 1) * LANES)
    qq = [_split_heads(q_ref[0, :, lanes(g)]) for g in pairs]
    carry_ref[...] = jnp.zeros(carry_ref.shape, F32)
    acc_ref[...] = jnp.zeros(acc_ref.shape, F32)
    kk, qc = _tile_ids((blk, blk))
    later = jnp.where(qc > kk, 1.0, 0.0).astype(BF16)
    causal = jnp.concatenate([kk < qc, kk < qc], axis=1)

    def update(kj, diagonal):
        start = pl.multiple_of(kj * blk, blk)
        logits = []
        for g in pairs:
            z = _nt_dot(k_ref[0, pl.ds(start, blk), lanes(g)], qq[g])
            log_1m = -(jnp.maximum(z, 0.0) + jnp.log(1.0 + jnp.exp(-jnp.abs(z))))
            if diagonal:
                log_1m = jnp.where(causal, log_1m, 0.0)
            hi = log_1m.astype(BF16)
            lo = (log_1m - hi.astype(F32)).astype(BF16)
            remain = (jnp.dot(later, hi, preferred_element_type=F32)
                      + jnp.dot(later, lo, preferred_element_type=F32))
            carry = carry_ref[g]
            logits.append(z + log_1m + remain + carry)
            carry_ref[g] = carry + (remain[:1] + log_1m[:1])
        for g in pairs:
            a = jnp.exp(logits[g])
            if diagonal:
                a = jnp.where(causal, a, 0.0)
            acc_ref[g] += jnp.dot(vt_ref[0, g, kj, :LANES, :], a.astype(BF16), preferred_element_type=F32)

    def live():
        return jnp.max(carry_ref[...]) > STICK_DEAD_LOGIT

    update(qi, True)

    def body(state):
        j, _ = state
        update(qi - 1 - j, False)
        return j + 1, live()

    lax.while_loop(lambda st: jnp.logical_and(st[0] < qi, st[1]), body, (jnp.int32(0), live()))

    for g in pairs:
        acc = acc_ref[g]
        o = jnp.concatenate([acc[:HEAD_DIM, :blk], acc[HEAD_DIM:, blk:]], axis=0)
        o_ref[0, :, lanes(g)] = o.T.astype(BF16)


def _stick_attn(proj, vt):
    b, s, _ = proj.shape
    nq = s // ATT_BLOCK
    pps = STICK_PAIRS_PER_STEP
    width = pps * LANES
    base = 2 * A_WIDTH // width
    q_blk, k_blk = base, base + B_WIDTH // width
    vt_blk = A_HEADS // pps
    return pl.pallas_call(
        _stick_kernel,
        grid=(b, B_WIDTH // width, nq),
        in_specs=[
            pl.BlockSpec((1, ATT_BLOCK, width), lambda bi, g, qi: (bi, qi, q_blk + g)),
            pl.BlockSpec((1, s, width), lambda bi, g, qi: (bi, 0, k_blk + g)),
            pl.BlockSpec((1, pps, nq, LANES + ONES_ROWS, ATT_BLOCK), lambda bi, g, qi: (bi, vt_blk + g, 0, 0, 0)),
        ],
        out_specs=pl.BlockSpec((1, ATT_BLOCK, width), lambda bi, g, qi: (bi, qi, g)),
        out_shape=jax.ShapeDtypeStruct((b, s, B_WIDTH), BF16),
        scratch_shapes=[
            pltpu.VMEM((pps, 1, 2 * ATT_BLOCK), F32),
            pltpu.VMEM((pps, LANES, 2 * ATT_BLOCK), F32),
        ],
        compiler_params=pltpu.CompilerParams(
            dimension_semantics=("parallel", "parallel", "arbitrary"), vmem_limit_bytes=VMEM_LIMIT),
        name="stick_breaking",
    )(proj, proj, vt)


def _band_kernel(q_ref, k_ref, vt_ref, bias_ref, o_ref, s_ref):
    blk = ATT_BLOCK
    qi = pl.program_id(2)
    first_tile = jnp.maximum(qi - 2, 0)
    start = pl.multiple_of(first_tile * blk, blk)
    back = [qi - first_tile - w for w in range(3)]
    outside = [jnp.where(b >= 0, 0.0, -jnp.inf) for b in back]
    tile = [jnp.clip(b, 0, 2) for b in back]

    def score(g):
        lanes = slice(g * LANES, (g + 1) * LANES)
        qq = _split_heads(q_ref[0, :, lanes])
        s_ref[g] = _nt_dot(k_ref[0, pl.ds(start, 3 * blk), lanes], qq)
        tile_max = []
        for w in range(3):
            rows = slice(w * blk, (w + 1) * blk)
            bias = jnp.concatenate([bias_ref[2 * g, tile[w]], bias_ref[2 * g + 1, tile[w]]], axis=1)
            s = s_ref[g, rows, :] + (bias + outside[w])
            s_ref[g, rows, :] = s
            tile_max.append(jnp.max(s, axis=0, keepdims=True))
        return functools.reduce(jnp.maximum, tile_max)

    def attend(g, m):
        lanes = slice(g * LANES, (g + 1) * LANES)
        p = jnp.exp2(s_ref[g] - m).astype(BF16)
        vt = jnp.concatenate([vt_ref[0, g, first_tile + w] for w in range(3)], axis=1)
        acc = jnp.dot(vt, p, preferred_element_type=F32)
        o = acc[:LANES] / acc[LANES:LANES + 1]
        o = jnp.concatenate([o[:HEAD_DIM, :blk], o[HEAD_DIM:, blk:]], axis=0)
        o_ref[0, :, lanes] = o.T.astype(BF16)

    pending = score(0)
    for g in range(BAND_PAIRS_PER_STEP):
        following = score(g + 1) if g + 1 < BAND_PAIRS_PER_STEP else None
        attend(g, pending)
        pending = following


def _band_attn(proj, vt, bias_tab):
    b, s, _ = proj.shape
    nq = s // ATT_BLOCK
    pps = BAND_PAIRS_PER_STEP
    width = pps * LANES
    q_blk, k_blk = 0, C_WIDTH // width
    return pl.pallas_call(
        _band_kernel,
        grid=(b, C_WIDTH // width, nq),
        in_specs=[
            pl.BlockSpec((1, ATT_BLOCK, width), lambda bi, g, qi: (bi, qi, q_blk + g)),
            pl.BlockSpec((1, s, width), lambda bi, g, qi: (bi, 0, k_blk + g)),
            pl.BlockSpec((1, pps, nq, LANES + ONES_ROWS, ATT_BLOCK), lambda bi, g, qi: (bi, g, 0, 0, 0)),
            pl.BlockSpec((2 * pps, 3, ATT_BLOCK, ATT_BLOCK), lambda bi, g, qi: (g, 0, 0, 0)),
        ],
        out_specs=pl.BlockSpec((1, ATT_BLOCK, width), lambda bi, g, qi: (bi, qi, g)),
        out_shape=jax.ShapeDtypeStruct((b, s, C_WIDTH), BF16),
        scratch_shapes=[pltpu.VMEM((pps, 3 * ATT_BLOCK, 2 * ATT_BLOCK), F32)],
        compiler_params=pltpu.CompilerParams(
            dimension_semantics=("parallel", "parallel", "arbitrary"), vmem_limit_bytes=VMEM_LIMIT),
        name="chunk_band",
    )(proj, proj, vt, bias_tab)


def _t5_bucket(rel):
    nb = T5_BUCKETS // 2
    max_exact = nb // 2
    n = jnp.abs(rel)
    large = max_exact + (jnp.log(jnp.maximum(n, 1).astype(jnp.float32) / max_exact)
                         / math.log(T5_MAX_DIST / max_exact) * (nb - max_exact)).astype(jnp.int32)
    large = jnp.minimum(large, nb - 1)
    return jnp.where(rel > 0, nb, 0) + jnp.where(n < max_exact, n, large)


def _toeplitz_tiles(per_distance):
    blk = ATT_BLOCK
    vec = per_distance.T.astype(F32)
    h, period = vec.shape
    skew = jnp.tile(vec, (1, blk))[:, :blk * (period - 1)].reshape(h, blk, period - 1)
    return jnp.stack([skew[:, :, blk - 1 + blk * d:2 * blk - 1 + blk * d] for d in range(3)], axis=1)


def _tile_distances():
    return jnp.arange(4 * ATT_BLOCK) - (ATT_BLOCK - 1)


def _tile_chunks():
    i = jnp.arange(ATT_BLOCK) // CHUNK
    return i[:, None], i[None, :]


def _t5_bias_tiles(t5_bias):
    assert ATT_BLOCK >= T5_MAX_DIST
    tiles = _toeplitz_tiles(t5_bias[_t5_bucket(-_tile_distances())])
    far = tiles[:, 2, :1, :1]
    tiles = (tiles[:, :2] - far[:, None]) * LOG2_E
    key_chunk, query_chunk = _tile_chunks()
    return tiles.at[:, 0].add(jnp.where(key_chunk <= query_chunk, 0.0, -jnp.inf))


def _band_bias_tiles(rel_bias):
    assert C_LEFT_CHUNKS * CHUNK == 2 * ATT_BLOCK
    tiles = _toeplitz_tiles(rel_bias[jnp.clip(_tile_distances(), -C_MAX_REL, C_MAX_REL) + C_MAX_REL]) * LOG2_E
    key_chunk, query_chunk = _tile_chunks()
    tiles = tiles.at[:, 0].add(jnp.where(key_chunk <= query_chunk, 0.0, -jnp.inf))
    return tiles.at[:, 2].add(jnp.where(key_chunk >= query_chunk, 0.0, -jnp.inf))


def _chunk_flags(*sections):
    flags = []
    for width, normed in sections:
        flags += [normed] * (width // LANES)
    return tuple(flags)


def kernel(x, t5_bias, norm_mix, norm_ffn, ab_w_in, a_q_norm, a_k_norm, a_lambda, a_subln, ab_w_out,
           c_w_in, c_q_norm, c_k_norm, c_rel_bias, c_w_out, ffn_w_up, ffn_conv_w, ffn_conv_b, ffn_w_down):
    b, seq, d = x.shape
    depth = norm_mix.shape[0]
    assert d == D_MODEL and seq % ROW_TILE == 0 and seq % PROJ_ROW_TILE == 0 and seq % ATT_BLOCK == 0
    scale = HEAD_DIM ** -0.5
    h = x.reshape(b * seq, d)
    t5_tiles = _t5_bias_tiles(t5_bias)
    ones = lambda n: jnp.ones((n,), F32)
    even_flags = _chunk_flags((2 * A_WIDTH, True), (2 * B_WIDTH, False))
    odd_flags = _chunk_flags((2 * C_WIDTH, True))
    for layer in range(depth):
        if layer % 2 == 0:
            e = layer // 2
            lam_init = 0.8 - 0.6 * math.exp(-0.3 * layer)
            w_in = ab_w_in[e].astype(BF16)
            gain = jnp.concatenate([
                jnp.tile(a_q_norm[e], 2 * A_HEADS) * (scale * LOG2_E), jnp.tile(a_k_norm[e], 2 * A_HEADS),
                ones(B_WIDTH) * scale, ones(B_WIDTH)])
            qk_cols = 2 * A_WIDTH
            w_main = jnp.concatenate([w_in[:, :qk_cols], w_in[:, 3 * A_WIDTH:3 * A_WIDTH + 2 * B_WIDTH]], axis=1)
            w_v = jnp.concatenate([w_in[:, qk_cols:3 * A_WIDTH], w_in[:, 3 * A_WIDTH + 2 * B_WIDTH:]], axis=1)
            proj, vt = _norm_proj(h, norm_mix[layer], w_main, gain, even_flags, w_v.T, seq)
            proj = proj.reshape(b, seq, -1)
            lam_p = a_lambda[e]
            lam = jnp.exp(jnp.sum(lam_p[0] * lam_p[1])) - jnp.exp(jnp.sum(lam_p[2] * lam_p[3])) + lam_init
            ya = _diff_attn(proj, vt, t5_tiles, a_subln[e], lam.astype(F32), 1.0 - lam_init)
            yb = _stick_attn(proj, vt)
            ys = [ya.reshape(b * seq, A_WIDTH), yb.reshape(b * seq, B_WIDTH)]
            w_out = ab_w_out[e]
        else:
            o = layer // 2
            w_in = c_w_in[o].astype(BF16)
            gain = jnp.concatenate([
                jnp.tile(c_q_norm[o], C_HEADS) * (scale * LOG2_E), jnp.tile(c_k_norm[o], C_HEADS)])
            proj, vt = _norm_proj(h, norm_mix[layer], w_in[:, :2 * C_WIDTH], gain, odd_flags,
                                  w_in[:, 2 * C_WIDTH:].T, seq)
            proj = proj.reshape(b, seq, -1)
            y = _band_attn(proj, vt, _band_bias_tiles(c_rel_bias[o]))
            ys = [y.reshape(b * seq, C_WIDTH)]
            w_out = c_w_out[o]
        h = _mix_ffn(h, ys, w_out.astype(BF16), norm_ffn[layer], ffn_w_up[layer].astype(BF16), ffn_conv_w[layer],
                     ffn_conv_b[layer], ffn_w_down[layer].astype(BF16), seq)
    return h.reshape(b, seq, d)
```
